```python
import math
import jax, jax.numpy as jnp
from jax import lax
import numpy as np

D_MODEL = 1024
BATCH = 8
SEQ = 2048
DEPTH = 1
DEC_BATCH = 128
DEC_SEQ = 1
PAST_LEN = 16384
PAGE_SIZE = 128

MIX = D_MODEL
POOL_WIDTH = MIX // 2
POOL_WINDOWS = (2, 4, 8, 16)
POOL_GROUPS = len(POOL_WINDOWS)
POOL_GROUP_W = POOL_WIDTH // POOL_GROUPS
POOL_HIST = max(POOL_WINDOWS) - 1
CONV_CH = MIX - POOL_WIDTH
CONV_K = 31
CONV_HIST = CONV_K - 1
IN_COLS = POOL_WIDTH + 2 * CONV_CH
N_MEM = 256
MEM_HEADS = 4
MEM_HEAD_DIM = D_MODEL // MEM_HEADS
D_FF = -(-8 * D_MODEL // (3 * 256)) * 256
EPS = 1e-6

kernel_name = "hymba_pool_conformer_memxattn_step"


def rmsnorm(x, g):
    xf = x.astype(jnp.float32)
    y = xf * lax.rsqrt(jnp.mean(xf * xf, axis=-1, keepdims=True) + EPS)
    return (y * g.astype(jnp.float32)).astype(x.dtype)


def pool_mixer(a_ext, start_pos, w_map, b_map, scale):
    B, T, C = a_ext.shape
    L = T - POOL_HIST
    af = a_ext.astype(jnp.float32)
    csum = jnp.concatenate([jnp.zeros((B, 1, C), jnp.float32), jnp.cumsum(af, axis=1)], axis=1)
    pos = start_pos + jnp.arange(L)
    outs = []
    for g, w in enumerate(POOL_WINDOWS):
        sl = slice(g * POOL_GROUP_W, (g + 1) * POOL_GROUP_W)
        cg = csum[..., sl]
        win_sum = cg[:, POOL_HIST + 1:] - cg[:, POOL_HIST + 1 - w: POOL_HIST + 1 - w + L]
        cnt = jnp.minimum(w, pos + 1).astype(jnp.float32)[None, :, None]
        outs.append(win_sum / cnt - af[:, POOL_HIST:, sl])
    d = jnp.concatenate(outs, axis=-1).reshape(B, L, POOL_GROUPS, POOL_GROUP_W)
    y = jnp.einsum('blgc,gcd->blgd', d, w_map.astype(jnp.float32)).reshape(B, L, C)
    y = (y + b_map.astype(jnp.float32)) * scale.astype(jnp.float32)
    return y.astype(a_ext.dtype)


def conv_mixer(g_ext, w_dw, b_dw, ln_g, ln_b):
    y = lax.conv_general_dilated(g_ext, w_dw[:, None, :], window_strides=(1,), padding='VALID',
                                 dimension_numbers=('NWC', 'WIO', 'NWC'), feature_group_count=CONV_CH)
    yf = (y + b_dw).astype(jnp.float32)
    mu = jnp.mean(yf, axis=-1, keepdims=True)
    var = jnp.mean(jnp.square(yf - mu), axis=-1, keepdims=True)
    yn = (yf - mu) * lax.rsqrt(var + EPS) * ln_g.astype(jnp.float32) + ln_b.astype(jnp.float32)
    return jax.nn.silu(yn).astype(g_ext.dtype)


def mem_kv(mem, g_mem, w_k, w_v):
    B = mem.shape[0]
    m = rmsnorm(mem, g_mem)
    k = (m @ w_k).reshape(B, N_MEM, MEM_HEADS, MEM_HEAD_DIM)
    v = (m @ w_v).reshape(B, N_MEM, MEM_HEADS, MEM_HEAD_DIM)
    return k, v


def cross_attn(h, k, v, w_q, w_o):
    B, L, _ = h.shape
    q = (h @ w_q).reshape(B, L, MEM_HEADS, MEM_HEAD_DIM)
    s = jnp.einsum('blhd,bmhd->bhlm', q, k).astype(jnp.float32) / math.sqrt(MEM_HEAD_DIM)
    p = jax.nn.softmax(s, axis=-1).astype(v.dtype)
    o = jnp.einsum('bhlm,bmhd->blhd', p, v).reshape(B, L, D_MODEL)
    return o @ w_o


def layer(x, pool_hist, conv_hist, start_pos, mk, mv, g_mix, w_in, pool_map_w, pool_map_b, pool_scale,
          conv_dw_w, conv_dw_b, conv_ln_g, conv_ln_b, w_out, g_attn, w_q, w_o, g_ffn, w_gate, w_up, w_down):
    h = rmsnorm(x, g_mix)
    u = h @ w_in
    a = u[..., :POOL_WIDTH]
    val = u[..., POOL_WIDTH:POOL_WIDTH + CONV_CH]
    gate = u[..., POOL_WIDTH + CONV_CH:]
    glu = val * jax.nn.sigmoid(gate)
    a_ext = jnp.concatenate([pool_hist, a], axis=1)
    g_ext = jnp.concatenate([conv_hist, glu], axis=1)
    ya = pool_mixer(a_ext, start_pos, pool_map_w, pool_map_b, pool_scale)
    yb = conv_mixer(g_ext, conv_dw_w, conv_dw_b, conv_ln_g, conv_ln_b)
    x = x + jnp.concatenate([ya, yb], axis=-1) @ w_out
    x = x + cross_attn(rmsnorm(x, g_attn), mk, mv, w_q, w_o)
    h = rmsnorm(x, g_ffn)
    x = x + (jax.nn.silu(h @ w_gate) * (h @ w_up)) @ w_down
    return x, a_ext[:, -POOL_HIST:], g_ext[:, -CONV_HIST:]


def setup_inputs(seed: int = 0) -> dict:
    key = jax.random.key(seed)
    ks = jax.random.split(key, 32)
    f32 = jnp.float32

    def nrm(k, shape, scale=1.0):
        return jax.random.normal(k, shape, f32) * scale

    def gain(k, shape):
        return 1.0 + 0.05 * jax.random.normal(k, shape, f32)

    return {
        "x_prompt": nrm(ks[0], (BATCH, SEQ, D_MODEL)),
        "x_sample": nrm(ks[1], (DEC_BATCH, DEC_SEQ, D_MODEL)),
        "mem_prompt": nrm(ks[2], (BATCH, N_MEM, D_MODEL)),
        "state_pool": nrm(ks[3], (DEPTH, DEC_BATCH, POOL_HIST, POOL_WIDTH)),
        "state_conv": nrm(ks[4], (DEPTH, DEC_BATCH, CONV_HIST, CONV_CH), 0.5),
        "cache_mem_k": nrm(ks[5], (DEPTH, DEC_BATCH, N_MEM, MEM_HEADS, MEM_HEAD_DIM)),
        "cache_mem_v": nrm(ks[6], (DEPTH, DEC_BATCH, N_MEM, MEM_HEADS, MEM_HEAD_DIM)),
        "g_mix": gain(ks[7], (DEPTH, D_MODEL)),
        "w_in": nrm(ks[8], (DEPTH, D_MODEL, IN_COLS), D_MODEL ** -0.5),
        "pool_map_w": nrm(ks[9], (DEPTH, POOL_GROUPS, POOL_GROUP_W, POOL_GROUP_W), POOL_GROUP_W ** -0.5),
        "pool_map_b": nrm(ks[10], (DEPTH, POOL_WIDTH), 0.02),
        "pool_scale": gain(ks[11], (DEPTH, POOL_WIDTH)),
        "conv_dw_w": nrm(ks[12], (DEPTH, CONV_K, CONV_CH), CONV_K ** -0.5),
        "conv_dw_b": nrm(ks[13], (DEPTH, CONV_CH), 0.02),
        "conv_ln_g": gain(ks[14], (DEPTH, CONV_CH)),
        "conv_ln_b": nrm(ks[15], (DEPTH, CONV_CH), 0.02),
        "w_out": nrm(ks[16], (DEPTH, MIX, D_MODEL), MIX ** -0.5),
        "g_attn": gain(ks[17], (DEPTH, D_MODEL)),
        "g_mem": gain(ks[18], (DEPTH, D_MODEL)),
        "w_q": nrm(ks[19], (DEPTH, D_MODEL, D_MODEL), D_MODEL ** -0.5),
        "w_k": nrm(ks[20], (DEPTH, D_MODEL, D_MODEL), D_MODEL ** -0.5),
        "w_v": nrm(ks[21], (DEPTH, D_MODEL, D_MODEL), D_MODEL ** -0.5),
        "w_o": nrm(ks[22], (DEPTH, D_MODEL, D_MODEL), D_MODEL ** -0.5),
        "g_ffn": gain(ks[23], (DEPTH, D_MODEL)),
        "w_gate": nrm(ks[24], (DEPTH, D_MODEL, D_FF), D_MODEL ** -0.5),
        "w_up": nrm(ks[25], (DEPTH, D_MODEL, D_FF), D_MODEL ** -0.5),
        "w_down": nrm(ks[26], (DEPTH, D_FF, D_MODEL), D_FF ** -0.5),
        "g_final": gain(ks[27], (D_MODEL,)),
    }


def reference(x_prompt, x_sample, mem_prompt, state_pool, state_conv, cache_mem_k, cache_mem_v,
              g_mix, w_in, pool_map_w, pool_map_b, pool_scale, conv_dw_w, conv_dw_b, conv_ln_g, conv_ln_b,
              w_out, g_attn, g_mem, w_q, w_k, w_v, w_o, g_ffn, w_gate, w_up, w_down, g_final):
    xp, xs = x_prompt, x_sample
    pool_p, pool_s, conv_p, conv_s, mk_p, mv_p = [], [], [], [], [], []
    for l in range(DEPTH):
        shared = (g_mix[l], w_in[l], pool_map_w[l], pool_map_b[l], pool_scale[l], conv_dw_w[l], conv_dw_b[l],
                  conv_ln_g[l], conv_ln_b[l], w_out[l], g_attn[l], w_q[l], w_o[l], g_ffn[l], w_gate[l], w_up[l],
                  w_down[l])
        mk, mv = mem_kv(mem_prompt, g_mem[l], w_k[l], w_v[l])
        zp = jnp.zeros((xp.shape[0], POOL_HIST, POOL_WIDTH), xp.dtype)
        zc = jnp.zeros((xp.shape[0], CONV_HIST, CONV_CH), xp.dtype)
        xp, hp, cp = layer(xp, zp, zc, 0, mk, mv, *shared)
        xs, hs, cs = layer(xs, state_pool[l], state_conv[l], PAST_LEN, cache_mem_k[l], cache_mem_v[l], *shared)
        pool_p.append(hp); pool_s.append(hs); conv_p.append(cp); conv_s.append(cs)
        mk_p.append(mk); mv_p.append(mv)
    y_prompt = rmsnorm(xp, g_final)
    y_sample = rmsnorm(xs, g_final)
    return (y_prompt, y_sample, jnp.stack(pool_p), jnp.stack(pool_s), jnp.stack(conv_p), jnp.stack(conv_s),
            jnp.stack(mk_p), jnp.stack(mv_p))
```

```python
import functools
import math

import jax
import jax.numpy as jnp
from jax import lax
from jax.experimental import pallas as pl
from jax.experimental.pallas import tpu as pltpu

D_MODEL = 1024
POOL_WINDOWS = (2, 4, 8, 16)
POOL_GROUP_W = 128
POOL_WIDTH = 512
POOL_HIST = 15
CONV_CH = 512
CONV_K = 31
CONV_HIST = 30
IN_COLS = POOL_WIDTH + 2 * CONV_CH
N_MEM = 256
MEM_HEADS = 4
MEM_HEAD_DIM = 256
EPS = 1e-6
ATTN_SCALE = 1.0 / math.sqrt(MEM_HEAD_DIM)

SUBLANES = 8
POOL_PAD = 16
CONV_PAD = 32
VMEM_LIMIT = 56 * 1024 * 1024

BF16 = jnp.bfloat16
F32 = jnp.float32


def _rms(x, g):
    ms = jnp.mean(x * x, axis=-1, keepdims=True)
    return x * lax.rsqrt(ms + EPS) * g


def _dot(a, b):
    return jnp.dot(a, b, preferred_element_type=F32)


def _silu(x):
    return x * jax.nn.sigmoid(x)


def _layernorm_silu(y, g, b):
    mu = jnp.mean(y, axis=-1, keepdims=True)
    yc = y - mu
    var = jnp.mean(yc * yc, axis=-1, keepdims=True)
    return _silu(yc * lax.rsqrt(var + EPS) * g + b)


def _pool_map(d, wmap_ref, bmap, pscale):
    db = d.astype(BF16)
    y = jnp.concatenate([_dot(db[:, :256], wmap_ref[0]), _dot(db[:, 256:], wmap_ref[1])], axis=-1)
    return (y + bmap) * pscale


def _const_spec(shape):
    nd = len(shape)
    return pl.BlockSpec(shape, lambda *_: (0,) * nd, pipeline_mode=pl.Buffered(1))


def _mem_kv_kernel(mem_ref, g_ref, wk_ref, wv_ref, k_ref, v_ref, kt_ref, vb_ref):
    m = _rms(mem_ref[0], g_ref[...]).astype(BF16)
    k = _dot(m, wk_ref[...])
    v = _dot(m, wv_ref[...])
    k_ref[0] = k
    v_ref[0] = v
    kt_ref[0] = k.T.astype(BF16)
    vb_ref[0] = v.astype(BF16)


def _mem_kv(mem, g_mem, wk, wv):
    nb = mem.shape[0]
    blk = lambda shape: pl.BlockSpec(shape, lambda b: (b, 0, 0))
    return pl.pallas_call(
        _mem_kv_kernel,
        grid=(nb,),
        in_specs=[blk((1, N_MEM, D_MODEL)), _const_spec((1, D_MODEL)),
                  _const_spec((D_MODEL, D_MODEL)), _const_spec((D_MODEL, D_MODEL))],
        out_specs=[blk((1, N_MEM, D_MODEL)), blk((1, N_MEM, D_MODEL)),
                   blk((1, D_MODEL, N_MEM)), blk((1, N_MEM, D_MODEL))],
        out_shape=[jax.ShapeDtypeStruct((nb, N_MEM, D_MODEL), F32),
                   jax.ShapeDtypeStruct((nb, N_MEM, D_MODEL), F32),
                   jax.ShapeDtypeStruct((nb, D_MODEL, N_MEM), BF16),
                   jax.ShapeDtypeStruct((nb, N_MEM, D_MODEL), BF16)],
        compiler_params=pltpu.CompilerParams(dimension_semantics=("arbitrary",),
                                             vmem_limit_bytes=VMEM_LIMIT),
        name="mem_kv",
    )(mem, g_mem, wk, wv)


def _mixer_kernel(x_ref, kt_ref, vb_ref, gmix_ref, win_ref, wmap_ref, bmap_ref, pscale_ref,
                  wdw_ref, bdw_ref, lng_ref, lnb_ref, wout_ref, gattn_ref, wq_ref,
                  x1_ref, o_ref, pst_ref, cst_ref, a_ext, g_ext, *, tm):
    j = pl.program_id(1)

    @pl.when(j == 0)
    def _():
        a_ext[0:POOL_PAD, :] = jnp.zeros((POOL_PAD, POOL_WIDTH), F32)
        g_ext[0:CONV_PAD, :] = jnp.zeros((CONV_PAD, CONV_CH), F32)

    x = x_ref[0]
    h = _rms(x, gmix_ref[...]).astype(BF16)
    u = _dot(h, win_ref[...])
    a = u[:, :POOL_WIDTH]
    glu = u[:, POOL_WIDTH:POOL_WIDTH + CONV_CH] * jax.nn.sigmoid(u[:, POOL_WIDTH + CONV_CH:])
    a_ext[POOL_PAD:POOL_PAD + tm, :] = a
    g_ext[CONV_PAD:CONV_PAD + tm, :] = glu

    pos = j * tm + lax.broadcasted_iota(jnp.int32, (tm, 1), 0)
    ds = []
    for g, w in enumerate(POOL_WINDOWS):
        c0 = g * POOL_GROUP_W
        win = a[:, c0:c0 + POOL_GROUP_W]
        for i in range(1, w):
            win = win + a_ext[POOL_PAD - i:POOL_PAD - i + tm, c0:c0 + POOL_GROUP_W]
        cnt = jnp.minimum(w, pos + 1).astype(F32)
        ds.append(win / cnt - a[:, c0:c0 + POOL_GROUP_W])
    ya = _pool_map(jnp.concatenate(ds, axis=-1), wmap_ref, bmap_ref[...], pscale_ref[...])

    acc = jnp.zeros((tm, CONV_CH), F32)
    for k in range(CONV_K):
        r0 = CONV_PAD - CONV_HIST + k
        acc = acc + g_ext[r0:r0 + tm, :] * wdw_ref[k:k + 1, :]
    yb = _layernorm_silu(acc + bdw_ref[...], lng_ref[...], lnb_ref[...])

    pst_ref[0] = a_ext[POOL_PAD + tm - POOL_HIST:POOL_PAD + tm, :]
    cst_ref[0] = g_ext[CONV_PAD + tm - CONV_HIST:CONV_PAD + tm, :]
    a_ext[0:POOL_PAD, :] = a_ext[tm:tm + POOL_PAD, :]
    g_ext[0:CONV_PAD, :] = g_ext[tm:tm + CONV_PAD, :]

    mix = jnp.concatenate([ya, yb], axis=-1).astype(BF16)
    x1 = x + _dot(mix, wout_ref[...])
    x1_ref[0] = x1

    q = (_dot(_rms(x1, gattn_ref[...]).astype(BF16), wq_ref[...]) * ATTN_SCALE).astype(BF16)
    for hd in range(MEM_HEADS):
        c0 = hd * MEM_HEAD_DIM
        s = _dot(q[:, c0:c0 + MEM_HEAD_DIM], kt_ref[0, c0:c0 + MEM_HEAD_DIM, :])
        e = jnp.exp(s - jnp.max(s, axis=-1, keepdims=True))
        p = (e / jnp.sum(e, axis=-1, keepdims=True)).astype(BF16)
        o_ref[0, :, c0:c0 + MEM_HEAD_DIM] = _dot(p, vb_ref[0, :, c0:c0 + MEM_HEAD_DIM]).astype(BF16)


def _mixer(x, kt, vb, w, tm):
    nb, seq, _ = x.shape
    nj = seq // tm
    row = lambda width: pl.BlockSpec((1, tm, width), lambda b, j: (b, j, 0))
    per_b = lambda shape: pl.BlockSpec(shape, lambda b, j: (b, 0, 0))
    consts = [w["g_mix"], w["w_in"], w["wmap"], w["b_map"], w["p_scale"], w["w_dw"], w["b_dw"],
              w["ln_g"], w["ln_b"], w["w_out"], w["g_attn"], w["w_q"]]
    return pl.pallas_call(
        functools.partial(_mixer_kernel, tm=tm),
        grid=(nb, nj),
        in_specs=[row(D_MODEL), per_b((1, D_MODEL, N_MEM)), per_b((1, N_MEM, D_MODEL))]
                 + [_const_spec(c.shape) for c in consts],
        out_specs=[row(D_MODEL), row(D_MODEL), per_b((1, POOL_HIST, POOL_WIDTH)),
                   per_b((1, CONV_HIST, CONV_CH))],
        out_shape=[jax.ShapeDtypeStruct((nb, seq, D_MODEL), F32),
                   jax.ShapeDtypeStruct((nb, seq, D_MODEL), BF16),
                   jax.ShapeDtypeStruct((nb, POOL_HIST, POOL_WIDTH), F32),
                   jax.ShapeDtypeStruct((nb, CONV_HIST, CONV_CH), F32)],
        scratch_shapes=[pltpu.VMEM((POOL_PAD + tm, POOL_WIDTH), F32),
                        pltpu.VMEM((CONV_PAD + tm, CONV_CH), F32)],
        compiler_params=pltpu.CompilerParams(dimension_semantics=("arbitrary", "arbitrary"),
                                             vmem_limit_bytes=VMEM_LIMIT),
        name="mixer",
    )(x, kt, vb, *consts)


def _sample_pre_kernel(x_ref, sp_ref, sc_ref, gmix_ref, win_ref, wmap_ref, bmap_ref, pscale_ref,
                       wdw_ref, bdw_ref, lng_ref, lnb_ref, wout_ref, gattn_ref, wq_ref,
                       x1_ref, q_ref, pst_ref, cst_ref):
    x = x_ref[...]
    h = _rms(x, gmix_ref[...]).astype(BF16)
    u = _dot(h, win_ref[...])
    a = u[:, :POOL_WIDTH]
    glu = u[:, POOL_WIDTH:POOL_WIDTH + CONV_CH] * jax.nn.sigmoid(u[:, POOL_WIDTH + CONV_CH:])

    ds = []
    for g, w in enumerate(POOL_WINDOWS):
        c0 = g * POOL_GROUP_W
        a_g = a[:, c0:c0 + POOL_GROUP_W]
        hist = jnp.sum(sp_ref[:, POOL_HIST - (w - 1):POOL_HIST, c0:c0 + POOL_GROUP_W], axis=1)
        ds.append((hist + a_g) / float(w) - a_g)
    ya = _pool_map(jnp.concatenate(ds, axis=-1), wmap_ref, bmap_ref[...], pscale_ref[...])

    conv = jnp.sum(sc_ref[...] * wdw_ref[0:CONV_HIST, :][None], axis=1)
    conv = conv + glu * wdw_ref[CONV_HIST:CONV_K, :] + bdw_ref[...]
    yb = _layernorm_silu(conv, lng_ref[...], lnb_ref[...])

    pst_ref[:, 0:POOL_HIST - 1, :] = sp_ref[:, 1:POOL_HIST, :]
    pst_ref[:, POOL_HIST - 1:POOL_HIST, :] = a[:, None, :]
    cst_ref[:, 0:CONV_HIST - 1, :] = sc_ref[:, 1:CONV_HIST, :]
    cst_ref[:, CONV_HIST - 1:CONV_HIST, :] = glu[:, None, :]

    mix = jnp.concatenate([ya, yb], axis=-1).astype(BF16)
    x1 = x + _dot(mix, wout_ref[...])
    x1_ref[...] = x1
    q_ref[...] = _dot(_rms(x1, gattn_ref[...]).astype(BF16), wq_ref[...]) * ATTN_SCALE


def _sample_pre(x, sp, sc, w):
    n = x.shape[0]
    consts = [w["g_mix"], w["w_in"], w["wmap"], w["b_map"], w["p_scale"], w["w_dw"], w["b_dw"],
              w["ln_g"], w["ln_b"], w["w_out"], w["g_attn"], w["w_q"]]
    args = [x, sp, sc] + consts
    return pl.pallas_call(
        _sample_pre_kernel,
        grid=(1,),
        in_specs=[_const_spec(t.shape) for t in args],
        out_specs=[_const_spec((n, D_MODEL)), _const_spec((n, D_MODEL)),
                   _const_spec((n, POOL_HIST, POOL_WIDTH)), _const_spec((n, CONV_HIST, CONV_CH))],
        out_shape=[jax.ShapeDtypeStruct((n, D_MODEL), F32), jax.ShapeDtypeStruct((n, D_MODEL), F32),
                   jax.ShapeDtypeStruct((n, POOL_HIST, POOL_WIDTH), F32),
                   jax.ShapeDtypeStruct((n, CONV_HIST, CONV_CH), F32)],
        compiler_params=pltpu.CompilerParams(dimension_semantics=("arbitrary",),
                                             vmem_limit_bytes=VMEM_LIMIT),
        name="sample_pre",
    )(*args)


def _split_bf16(x):
    hi = x.astype(BF16)
    lo = (x - hi.astype(F32)).astype(BF16)
    return hi, lo


def _sample_attn_kernel(q_ref, k_ref, v_ref, o_ref, *, sb):
    lanes = 128
    col_head = lax.broadcasted_iota(jnp.int32, (D_MODEL, lanes), 0) // MEM_HEAD_DIM
    seg = (col_head == lax.broadcasted_iota(jnp.int32, (D_MODEL, lanes), 1)).astype(BF16)
    row_head = lax.broadcasted_iota(jnp.int32, (lanes, D_MODEL), 1) // MEM_HEAD_DIM
    seg_t = (row_head == lax.broadcasted_iota(jnp.int32, (lanes, D_MODEL), 0)).astype(BF16)

    def body(i, carry):
        kq_hi, kq_lo = _split_bf16(k_ref[i] * q_ref[i])
        s = _dot(kq_hi, seg) + _dot(kq_lo, seg)
        e = jnp.exp(s - jnp.max(s, axis=0, keepdims=True))
        p = e / jnp.sum(e, axis=0, keepdims=True)
        p_hi, p_lo = _split_bf16(p)
        pb = _dot(p_hi, seg_t) + _dot(p_lo, seg_t)
        o_ref[i] = jnp.sum(pb * v_ref[i], axis=0, keepdims=True)
        return carry

    lax.fori_loop(0, sb, body, 0)


def _sample_attn(q, k, v, sb):
    n = q.shape[0]
    q3 = q.reshape(n, 1, D_MODEL)
    blk = lambda rows: pl.BlockSpec((sb, rows, D_MODEL), lambda i: (i, 0, 0))
    o = pl.pallas_call(
        functools.partial(_sample_attn_kernel, sb=sb),
        grid=(n // sb,),
        in_specs=[blk(1), blk(N_MEM), blk(N_MEM)],
        out_specs=blk(1),
        out_shape=jax.ShapeDtypeStruct((n, 1, D_MODEL), F32),
        compiler_params=pltpu.CompilerParams(dimension_semantics=("arbitrary",),
                                             vmem_limit_bytes=VMEM_LIMIT),
        name="sample_attn",
    )(q3, k, v)
    return o.reshape(n, D_MODEL)


def _post_kernel(x1_ref, o_ref, wo_ref, gffn_ref, wg_ref, wu_ref, wd_ref, gfin_ref, y_ref):
    x2 = x1_ref[...] + _dot(o_ref[...].astype(BF16), wo_ref[...])
    h = _rms(x2, gffn_ref[...]).astype(BF16)
    act = (_silu(_dot(h, wg_ref[...])) * _dot(h, wu_ref[...])).astype(BF16)
    x3 = x2 + _dot(act, wd_ref[...])
    y_ref[...] = _rms(x3, gfin_ref[...])


def _post(x1, o, w, tm):
    n = x1.shape[0]
    row = pl.BlockSpec((tm, D_MODEL), lambda i: (i, 0))
    consts = [w["w_o"], w["g_ffn"], w["w_gate"], w["w_up"], w["w_down"], w["g_final"]]
    return pl.pallas_call(
        _post_kernel,
        grid=(n // tm,),
        in_specs=[row, row] + [_const_spec(c.shape) for c in consts],
        out_specs=row,
        out_shape=jax.ShapeDtypeStruct((n, D_MODEL), F32),
        compiler_params=pltpu.CompilerParams(dimension_semantics=("arbitrary",),
                                             vmem_limit_bytes=VMEM_LIMIT),
        name="post",
    )(x1, o, *consts)


def _prep_weights(g_mix, w_in, pool_map_w, pool_map_b, pool_scale, conv_dw_w, conv_dw_b, conv_ln_g,
                  conv_ln_b, w_out, g_attn, g_mem, w_q, w_k, w_v, w_o, g_ffn, w_gate, w_up, w_down,
                  g_final, l):
    vec = lambda v: v.reshape(1, -1)
    pm = pool_map_w[l]
    z = jnp.zeros((POOL_GROUP_W, POOL_GROUP_W), F32)
    wmap = jnp.stack([jnp.block([[pm[0], z], [z, pm[1]]]), jnp.block([[pm[2], z], [z, pm[3]]])])
    return dict(
        g_mix=vec(g_mix[l]), w_in=w_in[l].astype(BF16), wmap=wmap.astype(BF16), b_map=vec(pool_map_b[l]),
        p_scale=vec(pool_scale[l]), w_dw=conv_dw_w[l], b_dw=vec(conv_dw_b[l]), ln_g=vec(conv_ln_g[l]),
        ln_b=vec(conv_ln_b[l]), w_out=w_out[l].astype(BF16), g_attn=vec(g_attn[l]), g_mem=vec(g_mem[l]),
        w_q=w_q[l].astype(BF16), w_k=w_k[l].astype(BF16), w_v=w_v[l].astype(BF16),
        w_o=w_o[l].astype(BF16), g_ffn=vec(g_ffn[l]), w_gate=w_gate[l].astype(BF16),
        w_up=w_up[l].astype(BF16), w_down=w_down[l].astype(BF16), g_final=vec(g_final))


def kernel(x_prompt, x_sample, mem_prompt, state_pool, state_conv, cache_mem_k, cache_mem_v, g_mix, w_in,
           pool_map_w, pool_map_b, pool_scale, conv_dw_w, conv_dw_b, conv_ln_g, conv_ln_b, w_out, g_attn,
           g_mem, w_q, w_k, w_v, w_o, g_ffn, w_gate, w_up, w_down, g_final):
    assert state_pool.shape[0] == 1, "single-layer trunk"
    nb, seq, _ = x_prompt.shape
    ns = x_sample.shape[0]
    w = _prep_weights(g_mix, w_in, pool_map_w, pool_map_b, pool_scale, conv_dw_w, conv_dw_b, conv_ln_g,
                      conv_ln_b, w_out, g_attn, g_mem, w_q, w_k, w_v, w_o, g_ffn, w_gate, w_up, w_down,
                      g_final, 0)

    mk, mv, kt, vb = _mem_kv(mem_prompt, w["g_mem"], w["w_k"], w["w_v"])
    x1p, op, pool_p, conv_p = _mixer(x_prompt, kt, vb, w, tm=256)
    yp = _post(x1p.reshape(nb * seq, D_MODEL), op.reshape(nb * seq, D_MODEL), w, tm=256)

    x1s, qs, pool_s, conv_s = _sample_pre(x_sample.reshape(ns, D_MODEL), state_pool[0], state_conv[0], w)
    os_ = _sample_attn(qs, cache_mem_k[0].reshape(ns, N_MEM, D_MODEL),
                       cache_mem_v[0].reshape(ns, N_MEM, D_MODEL), sb=8)
    ys = _post(x1s, os_, w, tm=ns)

    kv_shape = (1, nb, N_MEM, MEM_HEADS, MEM_HEAD_DIM)
    return (yp.reshape(nb, seq, D_MODEL), ys.reshape(ns, 1, D_MODEL),
            pool_p[None], pool_s[None], conv_p[None], conv_s[None],
            mk.reshape(kv_shape), mv.reshape(kv_shape))
```

```python
import functools
import math

import jax
import jax.numpy as jnp
from jax import lax
from jax.experimental import pallas as pl
from jax.experimental.pallas import tpu as pltpu

D_MODEL = 1024
POOL_WINDOWS = (2, 4, 8, 16)
POOL_GROUP_W = 128
POOL_WIDTH = 512
POOL_HIST = 15
CONV_CH = 512
CONV_K = 31
CONV_HIST = 30
IN_COLS = POOL_WIDTH + 2 * CONV_CH
N_MEM = 256
MEM_HEADS = 4
MEM_HEAD_DIM = 256
EPS = 1e-6
ATTN_SCALE = 1.0 / math.sqrt(MEM_HEAD_DIM)

SUBLANES = 8
LANES = 128
POOL_SLABS = POOL_WIDTH // LANES
CONV_SLABS = CONV_CH // LANES
HEAD_ROWS = 2 * MEM_HEADS
CONV_ROWS = 64
POOL_PAD = 16
CONV_PAD = 32
VMEM_LIMIT = 56 * 1024 * 1024

BF16 = jnp.bfloat16
F32 = jnp.float32


def _rms(x, g):
    ms = jnp.mean(x * x, axis=-1, keepdims=True)
    return x * lax.rsqrt(ms + EPS) * g


def _dot(a, b):
    return jnp.dot(a, b, preferred_element_type=F32)


def _silu(x):
    return x * jax.nn.sigmoid(x)


def _layernorm_silu(y, g, b):
    mu = jnp.mean(y, axis=-1, keepdims=True)
    yc = y - mu
    var = jnp.mean(yc * yc, axis=-1, keepdims=True)
    return _silu(yc * lax.rsqrt(var + EPS) * g + b)


def _pool_map(d, wmap_ref, bmap, pscale):
    db = d.astype(BF16)
    y = jnp.concatenate([_dot(db[:, :256], wmap_ref[0]), _dot(db[:, 256:], wmap_ref[1])], axis=-1)
    return (y + bmap) * pscale


def _const_spec(shape):
    nd = len(shape)
    return pl.BlockSpec(shape, lambda *_: (0,) * nd, pipeline_mode=pl.Buffered(1))


def _mem_kv_kernel(mem_ref, g_ref, wk_ref, wv_ref, k_ref, v_ref, kt_ref, vb_ref):
    m = _rms(mem_ref[0], g_ref[...]).astype(BF16)
    k = _dot(m, wk_ref[...])
    v = _dot(m, wv_ref[...])
    for r in range(HEAD_ROWS):
        half, head = divmod(r, MEM_HEADS)
        c0 = head * MEM_HEAD_DIM + half * LANES
        k_ref[0, :, r, :] = k[:, c0:c0 + LANES]
        v_ref[0, :, r, :] = v[:, c0:c0 + LANES]
    kt_ref[0] = k.T.astype(BF16)
    vb_ref[0] = v.astype(BF16)


def _mem_kv(mem, g_mem, wk, wv):
    nb = mem.shape[0]
    blk = lambda shape: pl.BlockSpec(shape, lambda b: (b, 0, 0))
    rows_blk = pl.BlockSpec((1, N_MEM, HEAD_ROWS, LANES), lambda b: (b, 0, 0, 0))
    return pl.pallas_call(
        _mem_kv_kernel,
        grid=(nb,),
        in_specs=[blk((1, N_MEM, D_MODEL)), _const_spec((1, D_MODEL)),
                  _const_spec((D_MODEL, D_MODEL)), _const_spec((D_MODEL, D_MODEL))],
        out_specs=[rows_blk, rows_blk, blk((1, D_MODEL, N_MEM)), blk((1, N_MEM, D_MODEL))],
        out_shape=[jax.ShapeDtypeStruct((nb, N_MEM, HEAD_ROWS, LANES), F32),
                   jax.ShapeDtypeStruct((nb, N_MEM, HEAD_ROWS, LANES), F32),
                   jax.ShapeDtypeStruct((nb, D_MODEL, N_MEM), BF16),
                   jax.ShapeDtypeStruct((nb, N_MEM, D_MODEL), BF16)],
        compiler_params=pltpu.CompilerParams(dimension_semantics=("arbitrary",),
                                             vmem_limit_bytes=VMEM_LIMIT),
        name="mem_kv",
    )(mem, g_mem, wk, wv)


def _mixer_kernel(x_ref, kt_ref, vb_ref, gmix_ref, win_ref, wmap_ref, bmap_ref, pscale_ref,
                  wdw_ref, bdw_ref, lng_ref, lnb_ref, wout_ref, gattn_ref, wq_ref,
                  x1_ref, o_ref, pst_ref, cst_ref, a_ext, g_ext, conv_buf, *, tm):
    j = pl.program_id(1)

    @pl.when(j == 0)
    def _():
        a_ext[:, 0:POOL_PAD, :] = jnp.zeros((POOL_SLABS, POOL_PAD, LANES), F32)
        g_ext[:, 0:CONV_PAD, :] = jnp.zeros((CONV_SLABS, CONV_PAD, LANES), F32)

    x = x_ref[0]
    h = _rms(x, gmix_ref[...]).astype(BF16)
    u = _dot(h, win_ref[...])
    glu = u[:, POOL_WIDTH:POOL_WIDTH + CONV_CH] * jax.nn.sigmoid(u[:, POOL_WIDTH + CONV_CH:])
    for c in range(POOL_SLABS):
        a_ext[c, POOL_PAD:POOL_PAD + tm, :] = u[:, c * LANES:(c + 1) * LANES]
    for c in range(CONV_SLABS):
        g_ext[c, CONV_PAD:CONV_PAD + tm, :] = glu[:, c * LANES:(c + 1) * LANES]

    pos = j * tm + lax.broadcasted_iota(jnp.int32, (tm, 1), 0)
    ds = []
    for g, w in enumerate(POOL_WINDOWS):
        a_g = a_ext[g, POOL_PAD:POOL_PAD + tm, :]
        win = a_g
        for i in range(1, w):
            win = win + a_ext[g, POOL_PAD - i:POOL_PAD - i + tm, :]
        inv_cnt = 1.0 / jnp.minimum(w, pos + 1).astype(F32)
        ds.append(win * inv_cnt - a_g)
    ya = _pool_map(jnp.concatenate(ds, axis=-1), wmap_ref, bmap_ref[...], pscale_ref[...])

    for c in range(CONV_SLABS):
        for t0 in range(0, tm, CONV_ROWS):
            acc = None
            for k in range(CONV_K):
                r0 = CONV_PAD - CONV_HIST + t0 + k
                g = g_ext[c, r0:r0 + CONV_ROWS, :].reshape(CONV_ROWS // SUBLANES, SUBLANES, LANES)
                term = g * wdw_ref[k, :, c * LANES:(c + 1) * LANES][None]
                acc = term if acc is None else acc + term
            conv_buf[t0:t0 + CONV_ROWS, c * LANES:(c + 1) * LANES] = acc.reshape(CONV_ROWS, LANES)
    yb = _layernorm_silu(conv_buf[...] + bdw_ref[...], lng_ref[...], lnb_ref[...])

    for c in range(POOL_SLABS):
        pst_ref[0, :, c * LANES:(c + 1) * LANES] = a_ext[c, POOL_PAD + tm - POOL_HIST:POOL_PAD + tm, :]
        a_ext[c, 0:POOL_PAD, :] = a_ext[c, tm:tm + POOL_PAD, :]
    for c in range(CONV_SLABS):
        cst_ref[0, :, c * LANES:(c + 1) * LANES] = g_ext[c, CONV_PAD + tm - CONV_HIST:CONV_PAD + tm, :]
        g_ext[c, 0:CONV_PAD, :] = g_ext[c, tm:tm + CONV_PAD, :]

    mix = jnp.concatenate([ya, yb], axis=-1).astype(BF16)
    x1 = x + _dot(mix, wout_ref[...])
    x1_ref[0] = x1

    q = (_dot(_rms(x1, gattn_ref[...]).astype(BF16), wq_ref[...]) * ATTN_SCALE).astype(BF16)
    for hd in range(MEM_HEADS):
        c0 = hd * MEM_HEAD_DIM
        s = _dot(q[:, c0:c0 + MEM_HEAD_DIM], kt_ref[0, c0:c0 + MEM_HEAD_DIM, :])
        e = jnp.exp(s - jnp.max(s, axis=-1, keepdims=True))
        p = (e * (1.0 / jnp.sum(e, axis=-1, keepdims=True))).astype(BF16)
        o_ref[0, :, c0:c0 + MEM_HEAD_DIM] = _dot(p, vb_ref[0, :, c0:c0 + MEM_HEAD_DIM]).astype(BF16)


def _mixer(x, kt, vb, w, tm):
    nb, seq, _ = x.shape
    nj = seq // tm
    row = lambda width: pl.BlockSpec((1, tm, width), lambda b, j: (b, j, 0))
    per_b = lambda shape: pl.BlockSpec(shape, lambda b, j: (b, 0, 0))
    consts = [w["g_mix"], w["w_in"], w["wmap"], w["b_map"], w["p_scale"], w["w_dw"], w["b_dw"],
              w["ln_g"], w["ln_b"], w["w_out"], w["g_attn"], w["w_q"]]
    return pl.pallas_call(
        functools.partial(_mixer_kernel, tm=tm),
        grid=(nb, nj),
        in_specs=[row(D_MODEL), per_b((1, D_MODEL, N_MEM)), per_b((1, N_MEM, D_MODEL))]
                 + [_const_spec(c.shape) for c in consts],
        out_specs=[row(D_MODEL), row(D_MODEL), per_b((1, POOL_HIST, POOL_WIDTH)),
                   per_b((1, CONV_HIST, CONV_CH))],
        out_shape=[jax.ShapeDtypeStruct((nb, seq, D_MODEL), F32),
                   jax.ShapeDtypeStruct((nb, seq, D_MODEL), BF16),
                   jax.ShapeDtypeStruct((nb, POOL_HIST, POOL_WIDTH), F32),
                   jax.ShapeDtypeStruct((nb, CONV_HIST, CONV_CH), F32)],
        scratch_shapes=[pltpu.VMEM((POOL_SLABS, POOL_PAD + tm, LANES), F32),
                        pltpu.VMEM((CONV_SLABS, CONV_PAD + tm, LANES), F32),
                        pltpu.VMEM((tm, CONV_CH), F32)],
        compiler_params=pltpu.CompilerParams(dimension_semantics=("arbitrary", "arbitrary"),
                                             vmem_limit_bytes=VMEM_LIMIT),
        name="mixer",
    )(x, kt, vb, *consts)


def _sample_pre_kernel(x_ref, sp_ref, sc_ref, gmix_ref, win_ref, wmap_ref, bmap_ref, pscale_ref,
                       wdw_ref, bdw_ref, lng_ref, lnb_ref, wout_ref, gattn_ref, wq_ref,
                       x1_ref, q_ref, pst_ref, cst_ref):
    x = x_ref[...]
    h = _rms(x, gmix_ref[...]).astype(BF16)
    u = _dot(h, win_ref[...])
    a = u[:, :POOL_WIDTH]
    glu = u[:, POOL_WIDTH:POOL_WIDTH + CONV_CH] * jax.nn.sigmoid(u[:, POOL_WIDTH + CONV_CH:])

    ds = []
    for g, w in enumerate(POOL_WINDOWS):
        c0 = g * POOL_GROUP_W
        a_g = a[:, c0:c0 + POOL_GROUP_W]
        win = a_g
        for i in range(1, w):
            win = win + sp_ref[POOL_HIST - i, :, c0:c0 + POOL_GROUP_W]
        ds.append(win * (1.0 / w) - a_g)
    ya = _pool_map(jnp.concatenate(ds, axis=-1), wmap_ref, bmap_ref[...], pscale_ref[...])

    conv = glu * wdw_ref[CONV_HIST, 0:1, :] + bdw_ref[...]
    for k in range(CONV_HIST):
        conv = conv + sc_ref[k] * wdw_ref[k, 0:1, :]
    yb = _layernorm_silu(conv, lng_ref[...], lnb_ref[...])

    pst_ref[0:POOL_HIST - 1] = sp_ref[1:POOL_HIST]
    pst_ref[POOL_HIST - 1] = a
    cst_ref[0:CONV_HIST - 1] = sc_ref[1:CONV_HIST]
    cst_ref[CONV_HIST - 1] = glu

    mix = jnp.concatenate([ya, yb], axis=-1).astype(BF16)
    x1 = x + _dot(mix, wout_ref[...])
    x1_ref[...] = x1
    q_ref[...] = _dot(_rms(x1, gattn_ref[...]).astype(BF16), wq_ref[...]) * ATTN_SCALE


def _sample_pre(x, sp, sc, w):
    n = x.shape[0]
    consts = [w["g_mix"], w["w_in"], w["wmap"], w["b_map"], w["p_scale"], w["w_dw"], w["b_dw"],
              w["ln_g"], w["ln_b"], w["w_out"], w["g_attn"], w["w_q"]]
    args = [x, sp, sc] + consts
    return pl.pallas_call(
        _sample_pre_kernel,
        grid=(1,),
        in_specs=[_const_spec(t.shape) for t in args],
        out_specs=[_const_spec((n, D_MODEL)), _const_spec((n, D_MODEL)),
                   _const_spec((POOL_HIST, n, POOL_WIDTH)), _const_spec((CONV_HIST, n, CONV_CH))],
        out_shape=[jax.ShapeDtypeStruct((n, D_MODEL), F32), jax.ShapeDtypeStruct((n, D_MODEL), F32),
                   jax.ShapeDtypeStruct((POOL_HIST, n, POOL_WIDTH), F32),
                   jax.ShapeDtypeStruct((CONV_HIST, n, CONV_CH), F32)],
        compiler_params=pltpu.CompilerParams(dimension_semantics=("arbitrary",),
                                             vmem_limit_bytes=VMEM_LIMIT),
        name="sample_pre",
    )(*args)


def _sample_attn_kernel(q_ref, k_ref, v_ref, o_ref, *, sb):
    def body(i, carry):
        part = jnp.sum(k_ref[i] * q_ref[i][None], axis=-1, keepdims=True)
        s = part + pltpu.roll(part, MEM_HEADS, axis=1)
        e = jnp.exp(s - jnp.max(s, axis=0, keepdims=True))
        den = jnp.sum(e, axis=0)
        o_ref[i] = jnp.sum(e * v_ref[i], axis=0) / den
        return carry

    lax.fori_loop(0, sb, body, 0)


def _to_head_rows(t):
    lead = t.shape[:-2]
    t = t.reshape(lead + (MEM_HEADS, 2, LANES))
    return jnp.swapaxes(t, -3, -2).reshape(lead + (HEAD_ROWS, LANES))


def _from_head_rows(t):
    lead = t.shape[:-2]
    t = t.reshape(lead + (2, MEM_HEADS, LANES))
    return jnp.swapaxes(t, -3, -2).reshape(lead + (MEM_HEADS, MEM_HEAD_DIM))


def _sample_attn(q, k, v, sb):
    n = q.shape[0]
    blk = pl.BlockSpec((sb, N_MEM, HEAD_ROWS, LANES), lambda i: (i, 0, 0, 0))
    vec = pl.BlockSpec((sb, HEAD_ROWS, LANES), lambda i: (i, 0, 0))
    o = pl.pallas_call(
        functools.partial(_sample_attn_kernel, sb=sb),
        grid=(n // sb,),
        in_specs=[vec, blk, blk],
        out_specs=vec,
        out_shape=jax.ShapeDtypeStruct((n, HEAD_ROWS, LANES), F32),
        compiler_params=pltpu.CompilerParams(dimension_semantics=("arbitrary",),
                                             vmem_limit_bytes=VMEM_LIMIT),
        name="sample_attn",
    )(_to_head_rows(q.reshape(n, MEM_HEADS, MEM_HEAD_DIM)), _to_head_rows(k), _to_head_rows(v))
    return _from_head_rows(o).reshape(n, D_MODEL)


def _post_kernel(x1_ref, o_ref, wo_ref, gffn_ref, wg_ref, wu_ref, wd_ref, gfin_ref, y_ref):
    x2 = x1_ref[...] + _dot(o_ref[...].astype(BF16), wo_ref[...])
    h = _rms(x2, gffn_ref[...]).astype(BF16)
    act = (_silu(_dot(h, wg_ref[...])) * _dot(h, wu_ref[...])).astype(BF16)
    x3 = x2 + _dot(act, wd_ref[...])
    y_ref[...] = _rms(x3, gfin_ref[...])


def _post(x1, o, w, tm):
    n = x1.shape[0]
    row = pl.BlockSpec((tm, D_MODEL), lambda i: (i, 0))
    consts = [w["w_o"], w["g_ffn"], w["w_gate"], w["w_up"], w["w_down"], w["g_final"]]
    return pl.pallas_call(
        _post_kernel,
        grid=(n // tm,),
        in_specs=[row, row] + [_const_spec(c.shape) for c in consts],
        out_specs=row,
        out_shape=jax.ShapeDtypeStruct((n, D_MODEL), F32),
        compiler_params=pltpu.CompilerParams(dimension_semantics=("arbitrary",),
                                             vmem_limit_bytes=VMEM_LIMIT),
        name="post",
    )(x1, o, *consts)


def _prep_weights(g_mix, w_in, pool_map_w, pool_map_b, pool_scale, conv_dw_w, conv_dw_b, conv_ln_g,
                  conv_ln_b, w_out, g_attn, g_mem, w_q, w_k, w_v, w_o, g_ffn, w_gate, w_up, w_down,
                  g_final, l):
    vec = lambda v: v.reshape(1, -1)
    pm = pool_map_w[l]
    z = jnp.zeros((POOL_GROUP_W, POOL_GROUP_W), F32)
    wmap = jnp.stack([jnp.block([[pm[0], z], [z, pm[1]]]), jnp.block([[pm[2], z], [z, pm[3]]])])
    return dict(
        g_mix=vec(g_mix[l]), w_in=w_in[l].astype(BF16), wmap=wmap.astype(BF16), b_map=vec(pool_map_b[l]),
        p_scale=vec(pool_scale[l]),
        w_dw=jnp.broadcast_to(conv_dw_w[l][:, None, :], (CONV_K, SUBLANES, CONV_CH)),
        b_dw=vec(conv_dw_b[l]), ln_g=vec(conv_ln_g[l]), ln_b=vec(conv_ln_b[l]),
        w_out=w_out[l].astype(BF16), g_attn=vec(g_attn[l]), g_mem=vec(g_mem[l]),
        w_q=w_q[l].astype(BF16), w_k=w_k[l].astype(BF16), w_v=w_v[l].astype(BF16),
        w_o=w_o[l].astype(BF16), g_ffn=vec(g_ffn[l]), w_gate=w_gate[l].astype(BF16),
        w_up=w_up[l].astype(BF16), w_down=w_down[l].astype(BF16), g_final=vec(g_final))


def kernel(x_prompt, x_sample, mem_prompt, state_pool, state_conv, cache_mem_k, cache_mem_v, g_mix, w_in,
           pool_map_w, pool_map_b, pool_scale, conv_dw_w, conv_dw_b, conv_ln_g, conv_ln_b, w_out, g_attn,
           g_mem, w_q, w_k, w_v, w_o, g_ffn, w_gate, w_up, w_down, g_final):
    assert state_pool.shape[0] == 1, "single-layer trunk"
    nb, seq, _ = x_prompt.shape
    ns = x_sample.shape[0]
    w = _prep_weights(g_mix, w_in, pool_map_w, pool_map_b, pool_scale, conv_dw_w, conv_dw_b, conv_ln_g,
                      conv_ln_b, w_out, g_attn, g_mem, w_q, w_k, w_v, w_o, g_ffn, w_gate, w_up, w_down,
                      g_final, 0)

    mk, mv, kt, vb = _mem_kv(mem_prompt, w["g_mem"], w["w_k"], w["w_v"])
    x1p, op, pool_p, conv_p = _mixer(x_prompt, kt, vb, w, tm=512)
    yp = _post(x1p.reshape(nb * seq, D_MODEL), op.reshape(nb * seq, D_MODEL), w, tm=512)

    x1s, qs, pool_s, conv_s = _sample_pre(x_sample.reshape(ns, D_MODEL), jnp.swapaxes(state_pool[0], 0, 1),
                                          jnp.swapaxes(state_conv[0], 0, 1), w)
    pool_s = jnp.swapaxes(pool_s, 0, 1)
    conv_s = jnp.swapaxes(conv_s, 0, 1)
    os_ = _sample_attn(qs, cache_mem_k[0], cache_mem_v[0], sb=8)
    ys = _post(x1s, os_, w, tm=ns)

    return (yp.reshape(nb, seq, D_MODEL), ys.reshape(ns, 1, D_MODEL),
            pool_p[None], pool_s[None], conv_p[None], conv_s[None],
            _from_head_rows(mk)[None], _from_head_rows(mv)[None])
```

```python
import functools
import math

import jax
import jax.numpy as jnp
from jax import lax
from jax.experimental import pallas as pl
from jax.experimental.pallas import tpu as pltpu

D_MODEL = 1024
POOL_WINDOWS = (2, 4, 8, 16)
POOL_GROUP_W = 128
POOL_WIDTH = 512
POOL_HIST = 15
CONV_CH = 512
CONV_K = 31
CONV_HIST = 30
IN_COLS = POOL_WIDTH + 2 * CONV_CH
N_MEM = 256
MEM_HEADS = 4
MEM_HEAD_DIM = 256
EPS = 1e-6
ATTN_SCALE = 1.0 / math.sqrt(MEM_HEAD_DIM)

SUBLANES = 8
LANES = 128
POOL_SLABS = POOL_WIDTH // LANES
CONV_SLABS = CONV_CH // LANES
HEAD_ROWS = 2 * MEM_HEADS
CONV_ROWS = 64
POOL_PAD = 16
CONV_PAD = 32
VMEM_LIMIT = 60 * 1024 * 1024

BF16 = jnp.bfloat16
F32 = jnp.float32


def _rms(x, g):
    ms = jnp.mean(x * x, axis=-1, keepdims=True)
    return x * lax.rsqrt(ms + EPS) * g


def _dot(a, b):
    return jnp.dot(a, b, preferred_element_type=F32)


def _silu(x):
    return x * jax.nn.sigmoid(x)


def _layernorm_silu(y, g, b):
    mu = jnp.mean(y, axis=-1, keepdims=True)
    yc = y - mu
    var = jnp.mean(yc * yc, axis=-1, keepdims=True)
    return _silu(yc * lax.rsqrt(var + EPS) * g + b)


def _pool_map(d, wmap_ref, bmap, pscale):
    db = d.astype(BF16)
    y = jnp.concatenate([_dot(db[:, :256], wmap_ref[0]), _dot(db[:, 256:], wmap_ref[1])], axis=-1)
    return (y + bmap) * pscale


def _const_spec(shape):
    nd = len(shape)
    return pl.BlockSpec(shape, lambda *_: (0,) * nd, pipeline_mode=pl.Buffered(1))


def _mem_kv_kernel(mem_ref, g_ref, wk_ref, wv_ref, k_ref, v_ref, kt_ref, vb_ref):
    m = _rms(mem_ref[0], g_ref[...]).astype(BF16)
    k = _dot(m, wk_ref[...])
    v = _dot(m, wv_ref[...])
    for r in range(HEAD_ROWS):
        half, head = divmod(r, MEM_HEADS)
        c0 = head * MEM_HEAD_DIM + half * LANES
        k_ref[0, :, r, :] = k[:, c0:c0 + LANES]
        v_ref[0, :, r, :] = v[:, c0:c0 + LANES]
    kt_ref[0] = k.T.astype(BF16)
    vb_ref[0] = v.astype(BF16)


def _mem_kv(mem, g_mem, wk, wv):
    nb = mem.shape[0]
    blk = lambda shape: pl.BlockSpec(shape, lambda b: (b, 0, 0))
    rows_blk = pl.BlockSpec((1, N_MEM, HEAD_ROWS, LANES), lambda b: (b, 0, 0, 0))
    return pl.pallas_call(
        _mem_kv_kernel,
        grid=(nb,),
        in_specs=[blk((1, N_MEM, D_MODEL)), _const_spec((1, D_MODEL)),
                  _const_spec((D_MODEL, D_MODEL)), _const_spec((D_MODEL, D_MODEL))],
        out_specs=[rows_blk, rows_blk, blk((1, D_MODEL, N_MEM)), blk((1, N_MEM, D_MODEL))],
        out_shape=[jax.ShapeDtypeStruct((nb, N_MEM, HEAD_ROWS, LANES), F32),
                   jax.ShapeDtypeStruct((nb, N_MEM, HEAD_ROWS, LANES), F32),
                   jax.ShapeDtypeStruct((nb, D_MODEL, N_MEM), BF16),
                   jax.ShapeDtypeStruct((nb, N_MEM, D_MODEL), BF16)],
        compiler_params=pltpu.CompilerParams(dimension_semantics=("arbitrary",),
                                             vmem_limit_bytes=VMEM_LIMIT),
        name="mem_kv",
    )(mem, g_mem, wk, wv)


def _mixer_kernel(x_ref, xres_ref, kt_ref, vb_ref, gmix_ref, win_ref, wmap_ref, bmap_ref, pscale_ref,
                  wdw_ref, bdw_ref, lng_ref, lnb_ref, wout_ref, gattn_ref, wq_ref,
                  x1_ref, o_ref, pst_ref, cst_ref, a_ext, g_ext, conv_buf, mix_buf, *, tm, nj, n_tiles):
    s = pl.program_id(0)
    j = s % nj

    @pl.when(s == 0)
    def _():
        mix_buf[...] = jnp.zeros(mix_buf.shape, BF16)

    @pl.when(j == 0)
    def _():
        a_ext[:, 0:POOL_PAD, :] = jnp.zeros((POOL_SLABS, POOL_PAD, LANES), F32)
        g_ext[:, 0:CONV_PAD, :] = jnp.zeros((CONV_SLABS, CONV_PAD, LANES), F32)

    h = _rms(x_ref[...], gmix_ref[...]).astype(BF16)
    u = _dot(h, win_ref[...])
    glu = u[:, POOL_WIDTH:POOL_WIDTH + CONV_CH] * jax.nn.sigmoid(u[:, POOL_WIDTH + CONV_CH:])
    for c in range(POOL_SLABS):
        a_ext[c, POOL_PAD:POOL_PAD + tm, :] = u[:, c * LANES:(c + 1) * LANES]
    for c in range(CONV_SLABS):
        g_ext[c, CONV_PAD:CONV_PAD + tm, :] = glu[:, c * LANES:(c + 1) * LANES]

    x1 = xres_ref[...] + _dot(mix_buf[...], wout_ref[...])
    x1_ref[...] = x1
    q = (_dot(_rms(x1, gattn_ref[...]).astype(BF16), wq_ref[...]) * ATTN_SCALE).astype(BF16)
    for hd in range(MEM_HEADS):
        c0 = hd * MEM_HEAD_DIM
        sc = _dot(q[:, c0:c0 + MEM_HEAD_DIM], kt_ref[0, c0:c0 + MEM_HEAD_DIM, :])
        e = jnp.exp(sc - jnp.max(sc, axis=-1, keepdims=True))
        p = (e * (1.0 / jnp.sum(e, axis=-1, keepdims=True))).astype(BF16)
        o_ref[:, c0:c0 + MEM_HEAD_DIM] = _dot(p, vb_ref[0, :, c0:c0 + MEM_HEAD_DIM]).astype(BF16)

    pos = j * tm + lax.broadcasted_iota(jnp.int32, (tm, 1), 0)
    ds = []
    for g, w in enumerate(POOL_WINDOWS):
        a_g = a_ext[g, POOL_PAD:POOL_PAD + tm, :]
        win = a_g
        for i in range(1, w):
            win = win + a_ext[g, POOL_PAD - i:POOL_PAD - i + tm, :]
        inv_cnt = 1.0 / jnp.minimum(w, pos + 1).astype(F32)
        ds.append(win * inv_cnt - a_g)
    ya = _pool_map(jnp.concatenate(ds, axis=-1), wmap_ref, bmap_ref[...], pscale_ref[...])

    for c in range(CONV_SLABS):
        for t0 in range(0, tm, CONV_ROWS):
            acc = None
            for k in range(CONV_K):
                r0 = CONV_PAD - CONV_HIST + t0 + k
                g = g_ext[c, r0:r0 + CONV_ROWS, :].reshape(CONV_ROWS // SUBLANES, SUBLANES, LANES)
                term = g * wdw_ref[k, :, c * LANES:(c + 1) * LANES][None]
                acc = term if acc is None else acc + term
            conv_buf[t0:t0 + CONV_ROWS, c * LANES:(c + 1) * LANES] = acc.reshape(CONV_ROWS, LANES)
    yb = _layernorm_silu(conv_buf[...] + bdw_ref[...], lng_ref[...], lnb_ref[...])
    mix_buf[...] = jnp.concatenate([ya, yb], axis=-1).astype(BF16)

    @pl.when(s < n_tiles)
    def _():
        for c in range(POOL_SLABS):
            pst_ref[0, :, c * LANES:(c + 1) * LANES] = a_ext[c, POOL_PAD + tm - POOL_HIST:POOL_PAD + tm, :]
        for c in range(CONV_SLABS):
            cst_ref[0, :, c * LANES:(c + 1) * LANES] = g_ext[c, CONV_PAD + tm - CONV_HIST:CONV_PAD + tm, :]

    for c in range(POOL_SLABS):
        a_ext[c, 0:POOL_PAD, :] = a_ext[c, tm:tm + POOL_PAD, :]
    for c in range(CONV_SLABS):
        g_ext[c, 0:CONV_PAD, :] = g_ext[c, tm:tm + CONV_PAD, :]


def _mixer(x, kt, vb, w, tm):
    nb, seq, _ = x.shape
    nj = seq // tm
    n_tiles = nb * nj
    x2d = x.reshape(nb * seq, D_MODEL)
    cur = lambda s: jnp.minimum(s, n_tiles - 1)
    prev = lambda s: jnp.maximum(s - 1, 0)
    row_cur = pl.BlockSpec((tm, D_MODEL), lambda s: (cur(s), 0))
    row_prev = pl.BlockSpec((tm, D_MODEL), lambda s: (prev(s), 0))
    per_b = lambda shape, tile: pl.BlockSpec(shape, lambda s: (tile(s) // nj, 0, 0))
    consts = [w["g_mix"], w["w_in"], w["wmap"], w["b_map"], w["p_scale"], w["w_dw"], w["b_dw"],
              w["ln_g"], w["ln_b"], w["w_out"], w["g_attn"], w["w_q"]]
    x1, o, pst, cst = pl.pallas_call(
        functools.partial(_mixer_kernel, tm=tm, nj=nj, n_tiles=n_tiles),
        grid=(n_tiles + 1,),
        in_specs=[row_cur, row_prev, per_b((1, D_MODEL, N_MEM), prev), per_b((1, N_MEM, D_MODEL), prev)]
                 + [_const_spec(c.shape) for c in consts],
        out_specs=[row_prev, row_prev, per_b((1, POOL_HIST, POOL_WIDTH), cur),
                   per_b((1, CONV_HIST, CONV_CH), cur)],
        out_shape=[jax.ShapeDtypeStruct((nb * seq, D_MODEL), F32),
                   jax.ShapeDtypeStruct((nb * seq, D_MODEL), BF16),
                   jax.ShapeDtypeStruct((nb, POOL_HIST, POOL_WIDTH), F32),
                   jax.ShapeDtypeStruct((nb, CONV_HIST, CONV_CH), F32)],
        scratch_shapes=[pltpu.VMEM((POOL_SLABS, POOL_PAD + tm, LANES), F32),
                        pltpu.VMEM((CONV_SLABS, CONV_PAD + tm, LANES), F32),
                        pltpu.VMEM((tm, CONV_CH), F32),
                        pltpu.VMEM((tm, D_MODEL), BF16)],
        compiler_params=pltpu.CompilerParams(dimension_semantics=("arbitrary",),
                                             vmem_limit_bytes=VMEM_LIMIT),
        name="mixer",
    )(x2d, x2d, kt, vb, *consts)
    return x1, o, pst, cst


def _sample_pre_kernel(x_ref, sp_ref, sc_ref, gmix_ref, win_ref, wmap_ref, bmap_ref, pscale_ref,
                       wdw_ref, bdw_ref, lng_ref, lnb_ref, wout_ref, gattn_ref, wq_ref,
                       x1_ref, q_ref, pst_ref, cst_ref):
    x = x_ref[...]
    h = _rms(x, gmix_ref[...]).astype(BF16)
    u = _dot(h, win_ref[...])
    a = u[:, :POOL_WIDTH]
    glu = u[:, POOL_WIDTH:POOL_WIDTH + CONV_CH] * jax.nn.sigmoid(u[:, POOL_WIDTH + CONV_CH:])

    ds = []
    for g, w in enumerate(POOL_WINDOWS):
        c0 = g * POOL_GROUP_W
        a_g = a[:, c0:c0 + POOL_GROUP_W]
        win = a_g
        for i in range(1, w):
            win = win + sp_ref[POOL_HIST - i, :, c0:c0 + POOL_GROUP_W]
        ds.append(win * (1.0 / w) - a_g)
    ya = _pool_map(jnp.concatenate(ds, axis=-1), wmap_ref, bmap_ref[...], pscale_ref[...])

    conv = glu * wdw_ref[CONV_HIST, 0:1, :] + bdw_ref[...]
    for k in range(CONV_HIST):
        conv = conv + sc_ref[k] * wdw_ref[k, 0:1, :]
    yb = _layernorm_silu(conv, lng_ref[...], lnb_ref[...])

    pst_ref[0:POOL_HIST - 1] = sp_ref[1:POOL_HIST]
    pst_ref[POOL_HIST - 1] = a
    cst_ref[0:CONV_HIST - 1] = sc_ref[1:CONV_HIST]
    cst_ref[CONV_HIST - 1] = glu

    mix = jnp.concatenate([ya, yb], axis=-1).astype(BF16)
    x1 = x + _dot(mix, wout_ref[...])
    x1_ref[...] = x1
    q_ref[...] = _dot(_rms(x1, gattn_ref[...]).astype(BF16), wq_ref[...]) * ATTN_SCALE


def _sample_pre(x, sp, sc, w):
    n = x.shape[0]
    consts = [w["g_mix"], w["w_in"], w["wmap"], w["b_map"], w["p_scale"], w["w_dw"], w["b_dw"],
              w["ln_g"], w["ln_b"], w["w_out"], w["g_attn"], w["w_q"]]
    args = [x, sp, sc] + consts
    return pl.pallas_call(
        _sample_pre_kernel,
        grid=(1,),
        in_specs=[_const_spec(t.shape) for t in args],
        out_specs=[_const_spec((n, D_MODEL)), _const_spec((n, D_MODEL)),
                   _const_spec((POOL_HIST, n, POOL_WIDTH)), _const_spec((CONV_HIST, n, CONV_CH))],
        out_shape=[jax.ShapeDtypeStruct((n, D_MODEL), F32), jax.ShapeDtypeStruct((n, D_MODEL), F32),
                   jax.ShapeDtypeStruct((POOL_HIST, n, POOL_WIDTH), F32),
                   jax.ShapeDtypeStruct((CONV_HIST, n, CONV_CH), F32)],
        compiler_params=pltpu.CompilerParams(dimension_semantics=("arbitrary",),
                                             vmem_limit_bytes=VMEM_LIMIT),
        name="sample_pre",
    )(*args)


def _attend_one(q, k, v):
    kq = k * q[None]
    s = jnp.sum(kq + pltpu.roll(kq, MEM_HEADS, axis=1), axis=-1, keepdims=True)
    e = jnp.exp(s - jnp.max(s, axis=0, keepdims=True))
    return jnp.sum(e * v, axis=0) / jnp.sum(e, axis=0)


def _post_rows(x1_ref, o_ref, wo_ref, gffn_ref, wg_ref, wu_ref, wd_ref, gfin_ref, y_ref):
    x2 = x1_ref[...] + _dot(o_ref[...].astype(BF16), wo_ref[...])
    h = _rms(x2, gffn_ref[...]).astype(BF16)
    act = (_silu(_dot(h, wg_ref[...])) * _dot(h, wu_ref[...])).astype(BF16)
    x3 = x2 + _dot(act, wd_ref[...])
    y_ref[...] = _rms(x3, gfin_ref[...])


def _post_attn_kernel(x1_ref, o_ref, wo_ref, gffn_ref, wg_ref, wu_ref, wd_ref, gfin_ref, q_ref, k_ref, v_ref,
                      y_ref, os_ref, *, sb):
    for i in range(sb):
        os_ref[i] = _attend_one(q_ref[i], k_ref[i], v_ref[i])
    _post_rows(x1_ref, o_ref, wo_ref, gffn_ref, wg_ref, wu_ref, wd_ref, gfin_ref, y_ref)


def _to_head_rows(t):
    lead = t.shape[:-2]
    t = t.reshape(lead + (MEM_HEADS, 2, LANES))
    return jnp.swapaxes(t, -3, -2).reshape(lead + (HEAD_ROWS, LANES))


def _from_head_rows(t):
    lead = t.shape[:-2]
    t = t.reshape(lead + (2, MEM_HEADS, LANES))
    return jnp.swapaxes(t, -3, -2).reshape(lead + (MEM_HEADS, MEM_HEAD_DIM))


def _post(x1, o, w, tm, attn=None):
    rows = x1.shape[0]
    steps = rows // tm
    row = pl.BlockSpec((tm, D_MODEL), lambda i: (i, 0))
    consts = [w["w_o"], w["g_ffn"], w["w_gate"], w["w_up"], w["w_down"], w["g_final"]]
    in_specs = [row, row] + [_const_spec(c.shape) for c in consts]
    params = pltpu.CompilerParams(dimension_semantics=("arbitrary",), vmem_limit_bytes=VMEM_LIMIT)
    y_shape = jax.ShapeDtypeStruct((rows, D_MODEL), F32)
    if attn is None:
        y = pl.pallas_call(_post_rows, grid=(steps,), in_specs=in_specs, out_specs=row, out_shape=y_shape,
                           compiler_params=params, name="post")(x1, o, *consts)
        return y, None
    q, k, v = attn
    n = q.shape[0]
    sb = n // steps
    assert sb * steps == n
    blk = pl.BlockSpec((sb, N_MEM, HEAD_ROWS, LANES), lambda i: (i, 0, 0, 0))
    vec = pl.BlockSpec((sb, HEAD_ROWS, LANES), lambda i: (i, 0, 0))
    y, os_ = pl.pallas_call(
        functools.partial(_post_attn_kernel, sb=sb),
        grid=(steps,),
        in_specs=in_specs + [vec, blk, blk],
        out_specs=[row, vec],
        out_shape=[y_shape, jax.ShapeDtypeStruct((n, HEAD_ROWS, LANES), F32)],
        compiler_params=params,
        name="post_attn",
    )(x1, o, *consts, _to_head_rows(q.reshape(n, MEM_HEADS, MEM_HEAD_DIM)), _to_head_rows(k), _to_head_rows(v))
    return y, _from_head_rows(os_).reshape(n, D_MODEL)


def _prep_weights(g_mix, w_in, pool_map_w, pool_map_b, pool_scale, conv_dw_w, conv_dw_b, conv_ln_g,
                  conv_ln_b, w_out, g_attn, g_mem, w_q, w_k, w_v, w_o, g_ffn, w_gate, w_up, w_down,
                  g_final, l):
    vec = lambda v: v.reshape(1, -1)
    pm = pool_map_w[l]
    z = jnp.zeros((POOL_GROUP_W, POOL_GROUP_W), F32)
    wmap = jnp.stack([jnp.block([[pm[0], z], [z, pm[1]]]), jnp.block([[pm[2], z], [z, pm[3]]])])
    return dict(
        g_mix=vec(g_mix[l]), w_in=w_in[l].astype(BF16), wmap=wmap.astype(BF16), b_map=vec(pool_map_b[l]),
        p_scale=vec(pool_scale[l]),
        w_dw=jnp.broadcast_to(conv_dw_w[l][:, None, :], (CONV_K, SUBLANES, CONV_CH)),
        b_dw=vec(conv_dw_b[l]), ln_g=vec(conv_ln_g[l]), ln_b=vec(conv_ln_b[l]),
        w_out=w_out[l].astype(BF16), g_attn=vec(g_attn[l]), g_mem=vec(g_mem[l]),
        w_q=w_q[l].astype(BF16), w_k=w_k[l].astype(BF16), w_v=w_v[l].astype(BF16),
        w_o=w_o[l].astype(BF16), g_ffn=vec(g_ffn[l]), w_gate=w_gate[l].astype(BF16),
        w_up=w_up[l].astype(BF16), w_down=w_down[l].astype(BF16), g_final=vec(g_final))


def kernel(x_prompt, x_sample, mem_prompt, state_pool, state_conv, cache_mem_k, cache_mem_v, g_mix, w_in,
           pool_map_w, pool_map_b, pool_scale, conv_dw_w, conv_dw_b, conv_ln_g, conv_ln_b, w_out, g_attn,
           g_mem, w_q, w_k, w_v, w_o, g_ffn, w_gate, w_up, w_down, g_final):
    assert state_pool.shape[0] == 1, "single-layer trunk"
    nb, seq, _ = x_prompt.shape
    ns = x_sample.shape[0]
    w = _prep_weights(g_mix, w_in, pool_map_w, pool_map_b, pool_scale, conv_dw_w, conv_dw_b, conv_ln_g,
                      conv_ln_b, w_out, g_attn, g_mem, w_q, w_k, w_v, w_o, g_ffn, w_gate, w_up, w_down,
                      g_final, 0)

    x1s, qs, pool_s, conv_s = _sample_pre(x_sample.reshape(ns, D_MODEL), jnp.swapaxes(state_pool[0], 0, 1),
                                          jnp.swapaxes(state_conv[0], 0, 1), w)
    pool_s = jnp.swapaxes(pool_s, 0, 1)
    conv_s = jnp.swapaxes(conv_s, 0, 1)

    mk, mv, kt, vb = _mem_kv(mem_prompt, w["g_mem"], w["w_k"], w["w_v"])
    x1p, op, pool_p, conv_p = _mixer(x_prompt, kt, vb, w, tm=512)
    yp, os_ = _post(x1p, op, w, tm=512, attn=(qs, cache_mem_k[0], cache_mem_v[0]))
    ys, _ = _post(x1s, os_, w, tm=ns)

    return (yp.reshape(nb, seq, D_MODEL), ys.reshape(ns, 1, D_MODEL),
            pool_p[None], pool_s[None], conv_p[None], conv_s[None],
            _from_head_rows(mk)[None], _from_head_rows(mv)[None])
```

```python
import functools
import math

import jax
import jax.numpy as jnp
from jax import lax
from jax.experimental import pallas as pl
from jax.experimental.pallas import tpu as pltpu

D_MODEL = 1024
POOL_WINDOWS = (2, 4, 8, 16)
POOL_GROUP_W = 128
POOL_WIDTH = 512
POOL_HIST = 15
CONV_CH = 512
CONV_K = 31
CONV_HIST = 30
IN_COLS = POOL_WIDTH + 2 * CONV_CH
N_MEM = 256
MEM_HEADS = 4
MEM_HEAD_DIM = 256
EPS = 1e-6
ATTN_SCALE = 1.0 / math.sqrt(MEM_HEAD_DIM)

SUBLANES = 8
LANES = 128
POOL_SLABS = POOL_WIDTH // LANES
CONV_SLABS = CONV_CH // LANES
HEAD_ROWS = 2 * MEM_HEADS
CONV_ROWS = 64
IN_ROWS = 256
POOL_PAD = 16
CONV_PAD = 32
VMEM_LIMIT = 60 * 1024 * 1024

BF16 = jnp.bfloat16
F32 = jnp.float32


def _rms(x, g):
    ms = jnp.mean(x * x, axis=-1, keepdims=True)
    return x * lax.rsqrt(ms + EPS) * g


def _dot(a, b):
    return jnp.dot(a, b, preferred_element_type=F32)


def _silu(x):
    return x * jax.nn.sigmoid(x)


def _layernorm_silu(y, g, b):
    mu = jnp.mean(y, axis=-1, keepdims=True)
    yc = y - mu
    var = jnp.mean(yc * yc, axis=-1, keepdims=True)
    return _silu(yc * lax.rsqrt(var + EPS) * g + b)


def _pool_map(d, wmap_ref, bmap, pscale):
    db = d.astype(BF16)
    y = jnp.concatenate([_dot(db[:, :256], wmap_ref[0]), _dot(db[:, 256:], wmap_ref[1])], axis=-1)
    return (y + bmap) * pscale


def _const_spec(shape):
    nd = len(shape)
    return pl.BlockSpec(shape, lambda *_: (0,) * nd, pipeline_mode=pl.Buffered(1))


def _mem_kv_kernel(mem_ref, g_ref, wk_ref, wv_ref, k_ref, v_ref, kt_ref, vb_ref):
    m = _rms(mem_ref[0], g_ref[...]).astype(BF16)
    k = _dot(m, wk_ref[...])
    v = _dot(m, wv_ref[...])
    for r in range(HEAD_ROWS):
        half, head = divmod(r, MEM_HEADS)
        c0 = head * MEM_HEAD_DIM + half * LANES
        k_ref[0, :, r, :] = k[:, c0:c0 + LANES]
        v_ref[0, :, r, :] = v[:, c0:c0 + LANES]
    kt_ref[0] = k.T.astype(BF16)
    vb_ref[0] = v.astype(BF16)


def _mem_kv(mem, g_mem, wk, wv):
    nb = mem.shape[0]
    blk = lambda shape: pl.BlockSpec(shape, lambda b: (b, 0, 0))
    rows_blk = pl.BlockSpec((1, N_MEM, HEAD_ROWS, LANES), lambda b: (b, 0, 0, 0))
    return pl.pallas_call(
        _mem_kv_kernel,
        grid=(nb,),
        in_specs=[blk((1, N_MEM, D_MODEL)), _const_spec((1, D_MODEL)),
                  _const_spec((D_MODEL, D_MODEL)), _const_spec((D_MODEL, D_MODEL))],
        out_specs=[rows_blk, rows_blk, blk((1, D_MODEL, N_MEM)), blk((1, N_MEM, D_MODEL))],
        out_shape=[jax.ShapeDtypeStruct((nb, N_MEM, HEAD_ROWS, LANES), F32),
                   jax.ShapeDtypeStruct((nb, N_MEM, HEAD_ROWS, LANES), F32),
                   jax.ShapeDtypeStruct((nb, D_MODEL, N_MEM), BF16),
                   jax.ShapeDtypeStruct((nb, N_MEM, D_MODEL), BF16)],
        compiler_params=pltpu.CompilerParams(dimension_semantics=("arbitrary",),
                                             vmem_limit_bytes=VMEM_LIMIT),
        name="mem_kv",
    )(mem, g_mem, wk, wv)


def _mixer_kernel(x_ref, xres_ref, kt_ref, vb_ref, gmix_ref, win_ref, wmap_ref, bmap_ref, pscale_ref,
                  wdw_ref, bdw_ref, lng_ref, lnb_ref, wout_ref, gattn_ref, wq_ref, wo_ref,
                  x2_ref, pst_ref, cst_ref, a_ext, g_ext, conv_buf, mix_buf, *, tm, nj, n_tiles):
    s = pl.program_id(0)
    j = s % nj

    @pl.when(s == 0)
    def _():
        mix_buf[...] = jnp.zeros(mix_buf.shape, BF16)

    @pl.when(j == 0)
    def _():
        a_ext[:, 0:POOL_PAD, :] = jnp.zeros((POOL_SLABS, POOL_PAD, LANES), F32)
        g_ext[:, 0:CONV_PAD, :] = jnp.zeros((CONV_SLABS, CONV_PAD, LANES), F32)

    for r0 in range(0, tm, IN_ROWS):
        h = _rms(x_ref[r0:r0 + IN_ROWS, :], gmix_ref[...]).astype(BF16)
        u = _dot(h, win_ref[...])
        glu = u[:, POOL_WIDTH:POOL_WIDTH + CONV_CH] * jax.nn.sigmoid(u[:, POOL_WIDTH + CONV_CH:])
        for c in range(POOL_SLABS):
            a_ext[c, POOL_PAD + r0:POOL_PAD + r0 + IN_ROWS, :] = u[:, c * LANES:(c + 1) * LANES]
        for c in range(CONV_SLABS):
            g_ext[c, CONV_PAD + r0:CONV_PAD + r0 + IN_ROWS, :] = glu[:, c * LANES:(c + 1) * LANES]

    x1 = xres_ref[...] + _dot(mix_buf[...], wout_ref[...])
    q = (_dot(_rms(x1, gattn_ref[...]).astype(BF16), wq_ref[...]) * ATTN_SCALE).astype(BF16)
    heads = []
    for hd in range(MEM_HEADS):
        c0 = hd * MEM_HEAD_DIM
        sc = _dot(q[:, c0:c0 + MEM_HEAD_DIM], kt_ref[0, c0:c0 + MEM_HEAD_DIM, :])
        e = jnp.exp(sc - jnp.max(sc, axis=-1, keepdims=True))
        p = (e * (1.0 / jnp.sum(e, axis=-1, keepdims=True))).astype(BF16)
        heads.append(_dot(p, vb_ref[0, :, c0:c0 + MEM_HEAD_DIM]).astype(BF16))
    x2_ref[...] = x1 + _dot(jnp.concatenate(heads, axis=-1), wo_ref[...])

    pos = j * tm + lax.broadcasted_iota(jnp.int32, (tm, 1), 0)
    ds = []
    for g, w in enumerate(POOL_WINDOWS):
        a_g = a_ext[g, POOL_PAD:POOL_PAD + tm, :]
        win = a_g
        for i in range(1, w):
            win = win + a_ext[g, POOL_PAD - i:POOL_PAD - i + tm, :]
        inv_cnt = 1.0 / jnp.minimum(w, pos + 1).astype(F32)
        ds.append(win * inv_cnt - a_g)
    ya = _pool_map(jnp.concatenate(ds, axis=-1), wmap_ref, bmap_ref[...], pscale_ref[...])

    for t0 in range(0, tm, CONV_ROWS):
        for c in range(CONV_SLABS):
            acc = None
            for k in range(CONV_K):
                r0 = CONV_PAD - CONV_HIST + t0 + k
                g = g_ext[c, r0:r0 + CONV_ROWS, :].reshape(CONV_ROWS // SUBLANES, SUBLANES, LANES)
                term = g * wdw_ref[k, :, c * LANES:(c + 1) * LANES][None]
                acc = term if acc is None else acc + term
            conv_buf[t0:t0 + CONV_ROWS, c * LANES:(c + 1) * LANES] = acc.reshape(CONV_ROWS, LANES)
    yb = _layernorm_silu(conv_buf[...] + bdw_ref[...], lng_ref[...], lnb_ref[...])
    mix_buf[...] = jnp.concatenate([ya, yb], axis=-1).astype(BF16)

    @pl.when(s < n_tiles)
    def _():
        for c in range(POOL_SLABS):
            pst_ref[0, :, c * LANES:(c + 1) * LANES] = a_ext[c, POOL_PAD + tm - POOL_HIST:POOL_PAD + tm, :]
        for c in range(CONV_SLABS):
            cst_ref[0, :, c * LANES:(c + 1) * LANES] = g_ext[c, CONV_PAD + tm - CONV_HIST:CONV_PAD + tm, :]

    for c in range(POOL_SLABS):
        a_ext[c, 0:POOL_PAD, :] = a_ext[c, tm:tm + POOL_PAD, :]
    for c in range(CONV_SLABS):
        g_ext[c, 0:CONV_PAD, :] = g_ext[c, tm:tm + CONV_PAD, :]


def _mixer(x, kt, vb, w, tm):
    nb, seq, _ = x.shape
    nj = seq // tm
    n_tiles = nb * nj
    x2d = x.reshape(nb * seq, D_MODEL)
    cur = lambda s: jnp.minimum(s, n_tiles - 1)
    prev = lambda s: jnp.maximum(s - 1, 0)
    row_cur = pl.BlockSpec((tm, D_MODEL), lambda s: (cur(s), 0))
    row_prev = pl.BlockSpec((tm, D_MODEL), lambda s: (prev(s), 0))
    per_b = lambda shape, tile: pl.BlockSpec(shape, lambda s: (tile(s) // nj, 0, 0))
    consts = [w["g_mix"], w["w_in"], w["wmap"], w["b_map"], w["p_scale"], w["w_dw"], w["b_dw"],
              w["ln_g"], w["ln_b"], w["w_out"], w["g_attn"], w["w_q"], w["w_o"]]
    return pl.pallas_call(
        functools.partial(_mixer_kernel, tm=tm, nj=nj, n_tiles=n_tiles),
        grid=(n_tiles + 1,),
        in_specs=[row_cur, row_prev, per_b((1, D_MODEL, N_MEM), prev), per_b((1, N_MEM, D_MODEL), prev)]
                 + [_const_spec(c.shape) for c in consts],
        out_specs=[row_prev, per_b((1, POOL_HIST, POOL_WIDTH), cur),
                   per_b((1, CONV_HIST, CONV_CH), cur)],
        out_shape=[jax.ShapeDtypeStruct((nb * seq, D_MODEL), F32),
                   jax.ShapeDtypeStruct((nb, POOL_HIST, POOL_WIDTH), F32),
                   jax.ShapeDtypeStruct((nb, CONV_HIST, CONV_CH), F32)],
        scratch_shapes=[pltpu.VMEM((POOL_SLABS, POOL_PAD + tm, LANES), F32),
                        pltpu.VMEM((CONV_SLABS, CONV_PAD + tm, LANES), F32),
                        pltpu.VMEM((tm, CONV_CH), F32),
                        pltpu.VMEM((tm, D_MODEL), BF16)],
        compiler_params=pltpu.CompilerParams(dimension_semantics=("arbitrary",),
                                             vmem_limit_bytes=VMEM_LIMIT),
        name="mixer",
    )(x2d, x2d, kt, vb, *consts)


def _sample_pre_kernel(x_ref, sp_ref, sc_ref, gmix_ref, win_ref, wmap_ref, bmap_ref, pscale_ref,
                       wdw_ref, bdw_ref, lng_ref, lnb_ref, wout_ref, gattn_ref, wq_ref,
                       x1_ref, q_ref, pst_ref, cst_ref):
    x = x_ref[...]
    h = _rms(x, gmix_ref[...]).astype(BF16)
    u = _dot(h, win_ref[...])
    a = u[:, :POOL_WIDTH]
    glu = u[:, POOL_WIDTH:POOL_WIDTH + CONV_CH] * jax.nn.sigmoid(u[:, POOL_WIDTH + CONV_CH:])

    ds = []
    for g, w in enumerate(POOL_WINDOWS):
        c0 = g * POOL_GROUP_W
        a_g = a[:, c0:c0 + POOL_GROUP_W]
        win = a_g
        for i in range(1, w):
            win = win + sp_ref[POOL_HIST - i, :, c0:c0 + POOL_GROUP_W]
        ds.append(win * (1.0 / w) - a_g)
    ya = _pool_map(jnp.concatenate(ds, axis=-1), wmap_ref, bmap_ref[...], pscale_ref[...])

    conv = glu * wdw_ref[CONV_HIST, 0:1, :] + bdw_ref[...]
    for k in range(CONV_HIST):
        conv = conv + sc_ref[k] * wdw_ref[k, 0:1, :]
    yb = _layernorm_silu(conv, lng_ref[...], lnb_ref[...])

    pst_ref[0:POOL_HIST - 1] = sp_ref[1:POOL_HIST]
    pst_ref[POOL_HIST - 1] = a
    cst_ref[0:CONV_HIST - 1] = sc_ref[1:CONV_HIST]
    cst_ref[CONV_HIST - 1] = glu

    mix = jnp.concatenate([ya, yb], axis=-1).astype(BF16)
    x1 = x + _dot(mix, wout_ref[...])
    x1_ref[...] = x1
    q_ref[...] = _dot(_rms(x1, gattn_ref[...]).astype(BF16), wq_ref[...]) * ATTN_SCALE


def _sample_pre(x, sp, sc, w):
    n = x.shape[0]
    consts = [w["g_mix"], w["w_in"], w["wmap"], w["b_map"], w["p_scale"], w["w_dw"], w["b_dw"],
              w["ln_g"], w["ln_b"], w["w_out"], w["g_attn"], w["w_q"]]
    args = [x, sp, sc] + consts
    return pl.pallas_call(
        _sample_pre_kernel,
        grid=(1,),
        in_specs=[_const_spec(t.shape) for t in args],
        out_specs=[_const_spec((n, D_MODEL)), _const_spec((n, D_MODEL)),
                   _const_spec((POOL_HIST, n, POOL_WIDTH)), _const_spec((CONV_HIST, n, CONV_CH))],
        out_shape=[jax.ShapeDtypeStruct((n, D_MODEL), F32), jax.ShapeDtypeStruct((n, D_MODEL), F32),
                   jax.ShapeDtypeStruct((POOL_HIST, n, POOL_WIDTH), F32),
                   jax.ShapeDtypeStruct((CONV_HIST, n, CONV_CH), F32)],
        compiler_params=pltpu.CompilerParams(dimension_semantics=("arbitrary",),
                                             vmem_limit_bytes=VMEM_LIMIT),
        name="sample_pre",
    )(*args)


def _attend_one(q, k, v):
    kq = k * q[None]
    s = jnp.sum(kq + pltpu.roll(kq, MEM_HEADS, axis=1), axis=-1, keepdims=True)
    e = jnp.exp(s - jnp.max(s, axis=0, keepdims=True))
    return jnp.sum(e * v, axis=0) / jnp.sum(e, axis=0)


def _ffn_rows(x2, gffn_ref, wg_ref, wu_ref, wd_ref, gfin_ref, y_ref):
    h = _rms(x2, gffn_ref[...]).astype(BF16)
    act = (_silu(_dot(h, wg_ref[...])) * _dot(h, wu_ref[...])).astype(BF16)
    x3 = x2 + _dot(act, wd_ref[...])
    y_ref[...] = _rms(x3, gfin_ref[...])


def _post_kernel(x1_ref, o_ref, wo_ref, gffn_ref, wg_ref, wu_ref, wd_ref, gfin_ref, y_ref):
    x2 = x1_ref[...] + _dot(o_ref[...].astype(BF16), wo_ref[...])
    _ffn_rows(x2, gffn_ref, wg_ref, wu_ref, wd_ref, gfin_ref, y_ref)


def _ffn_attn_kernel(x2_ref, gffn_ref, wg_ref, wu_ref, wd_ref, gfin_ref, q_ref, k_ref, v_ref,
                     y_ref, os_ref, *, sb):
    for i in range(sb):
        os_ref[i] = _attend_one(q_ref[i], k_ref[i], v_ref[i])
    _ffn_rows(x2_ref[...], gffn_ref, wg_ref, wu_ref, wd_ref, gfin_ref, y_ref)


def _to_head_rows(t):
    lead = t.shape[:-2]
    t = t.reshape(lead + (MEM_HEADS, 2, LANES))
    return jnp.swapaxes(t, -3, -2).reshape(lead + (HEAD_ROWS, LANES))


def _from_head_rows(t):
    lead = t.shape[:-2]
    t = t.reshape(lead + (2, MEM_HEADS, LANES))
    return jnp.swapaxes(t, -3, -2).reshape(lead + (MEM_HEADS, MEM_HEAD_DIM))


def _post_sample(x1, o, w):
    rows = x1.shape[0]
    args = [x1, o, w["w_o"], w["g_ffn"], w["w_gate"], w["w_up"], w["w_down"], w["g_final"]]
    return pl.pallas_call(
        _post_kernel, grid=(1,), in_specs=[_const_spec(a.shape) for a in args],
        out_specs=_const_spec((rows, D_MODEL)), out_shape=jax.ShapeDtypeStruct((rows, D_MODEL), F32),
        compiler_params=pltpu.CompilerParams(dimension_semantics=("arbitrary",), vmem_limit_bytes=VMEM_LIMIT),
        name="post")(*args)


def _ffn_attn(x2, w, tm, q, k, v):
    rows = x2.shape[0]
    steps = rows // tm
    n = q.shape[0]
    sb = n // steps
    assert sb * steps == n
    row = pl.BlockSpec((tm, D_MODEL), lambda i: (i, 0))
    consts = [w["g_ffn"], w["w_gate"], w["w_up"], w["w_down"], w["g_final"]]
    blk = pl.BlockSpec((sb, N_MEM, HEAD_ROWS, LANES), lambda i: (i, 0, 0, 0))
    vec = pl.BlockSpec((sb, HEAD_ROWS, LANES), lambda i: (i, 0, 0))
    y, os_ = pl.pallas_call(
        functools.partial(_ffn_attn_kernel, sb=sb),
        grid=(steps,),
        in_specs=[row] + [_const_spec(c.shape) for c in consts] + [vec, blk, blk],
        out_specs=[row, vec],
        out_shape=[jax.ShapeDtypeStruct((rows, D_MODEL), F32), jax.ShapeDtypeStruct((n, HEAD_ROWS, LANES), F32)],
        compiler_params=pltpu.CompilerParams(dimension_semantics=("arbitrary",), vmem_limit_bytes=VMEM_LIMIT),
        name="ffn_attn",
    )(x2, *consts, _to_head_rows(q.reshape(n, MEM_HEADS, MEM_HEAD_DIM)), _to_head_rows(k), _to_head_rows(v))
    return y, _from_head_rows(os_).reshape(n, D_MODEL)


def _prep_weights(g_mix, w_in, pool_map_w, pool_map_b, pool_scale, conv_dw_w, conv_dw_b, conv_ln_g,
                  conv_ln_b, w_out, g_attn, g_mem, w_q, w_k, w_v, w_o, g_ffn, w_gate, w_up, w_down,
                  g_final, l):
    vec = lambda v: v.reshape(1, -1)
    pm = pool_map_w[l]
    z = jnp.zeros((POOL_GROUP_W, POOL_GROUP_W), F32)
    wmap = jnp.stack([jnp.block([[pm[0], z], [z, pm[1]]]), jnp.block([[pm[2], z], [z, pm[3]]])])
    return dict(
        g_mix=vec(g_mix[l]), w_in=w_in[l].astype(BF16), wmap=wmap.astype(BF16), b_map=vec(pool_map_b[l]),
        p_scale=vec(pool_scale[l]),
        w_dw=jnp.broadcast_to(conv_dw_w[l][:, None, :], (CONV_K, SUBLANES, CONV_CH)),
        b_dw=vec(conv_dw_b[l]), ln_g=vec(conv_ln_g[l]), ln_b=vec(conv_ln_b[l]),
        w_out=w_out[l].astype(BF16), g_attn=vec(g_attn[l]), g_mem=vec(g_mem[l]),
        w_q=w_q[l].astype(BF16), w_k=w_k[l].astype(BF16), w_v=w_v[l].astype(BF16),
        w_o=w_o[l].astype(BF16), g_ffn=vec(g_ffn[l]), w_gate=w_gate[l].astype(BF16),
        w_up=w_up[l].astype(BF16), w_down=w_down[l].astype(BF16), g_final=vec(g_final))


def kernel(x_prompt, x_sample, mem_prompt, state_pool, state_conv, cache_mem_k, cache_mem_v, g_mix, w_in,
           pool_map_w, pool_map_b, pool_scale, conv_dw_w, conv_dw_b, conv_ln_g, conv_ln_b, w_out, g_attn,
           g_mem, w_q, w_k, w_v, w_o, g_ffn, w_gate, w_up, w_down, g_final):
    assert state_pool.shape[0] == 1, "single-layer trunk"
    nb, seq, _ = x_prompt.shape
    ns = x_sample.shape[0]
    w = _prep_weights(g_mix, w_in, pool_map_w, pool_map_b, pool_scale, conv_dw_w, conv_dw_b, conv_ln_g,
                      conv_ln_b, w_out, g_attn, g_mem, w_q, w_k, w_v, w_o, g_ffn, w_gate, w_up, w_down,
                      g_final, 0)

    x1s, qs, pool_s, conv_s = _sample_pre(x_sample.reshape(ns, D_MODEL), jnp.swapaxes(state_pool[0], 0, 1),
                                          jnp.swapaxes(state_conv[0], 0, 1), w)
    pool_s = jnp.swapaxes(pool_s, 0, 1)
    conv_s = jnp.swapaxes(conv_s, 0, 1)

    mk, mv, kt, vb = _mem_kv(mem_prompt, w["g_mem"], w["w_k"], w["w_v"])
    x2p, pool_p, conv_p = _mixer(x_prompt, kt, vb, w, tm=512)
    yp, os_ = _ffn_attn(x2p, w, 512, qs, cache_mem_k[0], cache_mem_v[0])
    ys = _post_sample(x1s, os_, w)

    return (yp.reshape(nb, seq, D_MODEL), ys.reshape(ns, 1, D_MODEL),
            pool_p[None], pool_s[None], conv_p[None], conv_s[None],
            _from_head_rows(mk)[None], _from_head_rows(mv)[None])
```

```python
import functools
import math

import jax
import jax.numpy as jnp
from jax import lax
from jax.experimental import pallas as pl
from jax.experimental.pallas import tpu as pltpu

D_MODEL = 1024
POOL_WINDOWS = (2, 4, 8, 16)
POOL_GROUP_W = 128
POOL_WIDTH = 512
POOL_HIST = 15
CONV_CH = 512
CONV_K = 31
CONV_HIST = 30
IN_COLS = POOL_WIDTH + 2 * CONV_CH
N_MEM = 256
MEM_HEADS = 4
MEM_HEAD_DIM = 256
EPS = 1e-6
ATTN_SCALE = 1.0 / math.sqrt(MEM_HEAD_DIM)
Q_SCALE = ATTN_SCALE * math.log2(math.e)

SUBLANES = 8
LANES = 128
POOL_SLABS = POOL_WIDTH // LANES
CONV_SLABS = CONV_CH // LANES
HEAD_ROWS = 2 * MEM_HEADS
CONV_ROWS = 64
IN_ROWS = 256
POOL_PAD = 16
CONV_PAD = 32
VMEM_LIMIT = 60 * 1024 * 1024

BF16 = jnp.bfloat16
F32 = jnp.float32


def _rms(x, g):
    ms = jnp.mean(x * x, axis=-1, keepdims=True)
    return x * lax.rsqrt(ms + EPS) * g


def _dot(a, b):
    return jnp.dot(a, b, preferred_element_type=F32)


def _silu(x):
    return x * jax.nn.sigmoid(x)


def _layernorm_silu(y, g, b):
    mu = jnp.mean(y, axis=-1, keepdims=True)
    yc = y - mu
    var = jnp.mean(yc * yc, axis=-1, keepdims=True)
    return _silu(yc * lax.rsqrt(var + EPS) * g + b)


def _pool_map(d, wmap_ref, bmap, pscale):
    db = d.astype(BF16)
    y = jnp.concatenate([_dot(db[:, :256], wmap_ref[0]), _dot(db[:, 256:], wmap_ref[1])], axis=-1)
    return (y + bmap) * pscale


def _const_spec(shape):
    nd = len(shape)
    return pl.BlockSpec(shape, lambda *_: (0,) * nd, pipeline_mode=pl.Buffered(1))


def _mem_kv_kernel(mem_ref, g_ref, wk_ref, wv_ref, k_ref, v_ref, kt_ref, vb_ref):
    m = _rms(mem_ref[0], g_ref[...]).astype(BF16)
    k = _dot(m, wk_ref[...])
    v = _dot(m, wv_ref[...])
    for r in range(HEAD_ROWS):
        half, head = divmod(r, MEM_HEADS)
        c0 = head * MEM_HEAD_DIM + half * LANES
        k_ref[0, pl.ds(r, N_MEM, stride=HEAD_ROWS), :] = k[:, c0:c0 + LANES]
        v_ref[0, pl.ds(r, N_MEM, stride=HEAD_ROWS), :] = v[:, c0:c0 + LANES]
    kt_ref[0] = k.T.astype(BF16)
    vb_ref[0] = v.astype(BF16)


def _mem_kv(mem, g_mem, wk, wv):
    nb = mem.shape[0]
    blk = lambda shape: pl.BlockSpec(shape, lambda b: (b, 0, 0))
    rows_blk = pl.BlockSpec((1, N_MEM * HEAD_ROWS, LANES), lambda b: (b, 0, 0))
    return pl.pallas_call(
        _mem_kv_kernel,
        grid=(nb,),
        in_specs=[blk((1, N_MEM, D_MODEL)), _const_spec((1, D_MODEL)),
                  _const_spec((D_MODEL, D_MODEL)), _const_spec((D_MODEL, D_MODEL))],
        out_specs=[rows_blk, rows_blk, blk((1, D_MODEL, N_MEM)), blk((1, N_MEM, D_MODEL))],
        out_shape=[jax.ShapeDtypeStruct((nb, N_MEM * HEAD_ROWS, LANES), F32),
                   jax.ShapeDtypeStruct((nb, N_MEM * HEAD_ROWS, LANES), F32),
                   jax.ShapeDtypeStruct((nb, D_MODEL, N_MEM), BF16),
                   jax.ShapeDtypeStruct((nb, N_MEM, D_MODEL), BF16)],
        compiler_params=pltpu.CompilerParams(dimension_semantics=("arbitrary",),
                                             vmem_limit_bytes=VMEM_LIMIT),
        name="mem_kv",
    )(mem, g_mem, wk, wv)


def _mixer_kernel(x_ref, xres_ref, kt_ref, vb_ref, gmix_ref, win_ref, wmap_ref, bmap_ref, pscale_ref,
                  wdw_ref, bdw_ref, lng_ref, lnb_ref, wout_ref, gattn_ref, wq_ref, wo_ref,
                  x2_ref, pst_ref, cst_ref, a_ext, g_ext, conv_buf, mix_buf, *, tm, nj, n_tiles):
    s = pl.program_id(0)
    j = s % nj

    @pl.when(s == 0)
    def _():
        mix_buf[...] = jnp.zeros(mix_buf.shape, BF16)

    @pl.when(j == 0)
    def _():
        a_ext[:, 0:POOL_PAD, :] = jnp.zeros((POOL_SLABS, POOL_PAD, LANES), F32)
        g_ext[:, 0:CONV_PAD, :] = jnp.zeros((CONV_SLABS, CONV_PAD, LANES), F32)

    for r0 in range(0, tm, IN_ROWS):
        h = _rms(x_ref[r0:r0 + IN_ROWS, :], gmix_ref[...]).astype(BF16)
        u = _dot(h, win_ref[...])
        glu = u[:, POOL_WIDTH:POOL_WIDTH + CONV_CH] * jax.nn.sigmoid(u[:, POOL_WIDTH + CONV_CH:])
        for c in range(POOL_SLABS):
            a_ext[c, POOL_PAD + r0:POOL_PAD + r0 + IN_ROWS, :] = u[:, c * LANES:(c + 1) * LANES]
        for c in range(CONV_SLABS):
            g_ext[c, CONV_PAD + r0:CONV_PAD + r0 + IN_ROWS, :] = glu[:, c * LANES:(c + 1) * LANES]

    x1 = xres_ref[...] + _dot(mix_buf[...], wout_ref[...])
    q = (_dot(_rms(x1, gattn_ref[...]).astype(BF16), wq_ref[...]) * Q_SCALE).astype(BF16)
    heads = []
    for hd in range(MEM_HEADS):
        c0 = hd * MEM_HEAD_DIM
        sc = _dot(q[:, c0:c0 + MEM_HEAD_DIM], kt_ref[0, c0:c0 + MEM_HEAD_DIM, :])
        e = jnp.exp2(sc - jnp.max(sc, axis=-1, keepdims=True))
        p = (e * (1.0 / jnp.sum(e, axis=-1, keepdims=True))).astype(BF16)
        heads.append(_dot(p, vb_ref[0, :, c0:c0 + MEM_HEAD_DIM]).astype(BF16))
    x2_ref[...] = x1 + _dot(jnp.concatenate(heads, axis=-1), wo_ref[...])

    pos = j * tm + lax.broadcasted_iota(jnp.int32, (tm, 1), 0)
    ds = []
    for g, w in enumerate(POOL_WINDOWS):
        a_g = a_ext[g, POOL_PAD:POOL_PAD + tm, :]
        win = a_g
        for i in range(1, w):
            win = win + a_ext[g, POOL_PAD - i:POOL_PAD - i + tm, :]
        inv_cnt = 1.0 / jnp.minimum(w, pos + 1).astype(F32)
        ds.append(win * inv_cnt - a_g)
    ya = _pool_map(jnp.concatenate(ds, axis=-1), wmap_ref, bmap_ref[...], pscale_ref[...])

    for t0 in range(0, tm, CONV_ROWS):
        for c in range(CONV_SLABS):
            acc = None
            for k in range(CONV_K):
                r0 = CONV_PAD - CONV_HIST + t0 + k
                g = g_ext[c, r0:r0 + CONV_ROWS, :].reshape(CONV_ROWS // SUBLANES, SUBLANES, LANES)
                term = g * wdw_ref[k, :, c * LANES:(c + 1) * LANES][None]
                acc = term if acc is None else acc + term
            conv_buf[t0:t0 + CONV_ROWS, c * LANES:(c + 1) * LANES] = acc.reshape(CONV_ROWS, LANES)
    yb = _layernorm_silu(conv_buf[...] + bdw_ref[...], lng_ref[...], lnb_ref[...])
    mix_buf[...] = jnp.concatenate([ya, yb], axis=-1).astype(BF16)

    @pl.when(s < n_tiles)
    def _():
        for c in range(POOL_SLABS):
            pst_ref[0, :, c * LANES:(c + 1) * LANES] = a_ext[c, POOL_PAD + tm - POOL_HIST:POOL_PAD + tm, :]
        for c in range(CONV_SLABS):
            cst_ref[0, :, c * LANES:(c + 1) * LANES] = g_ext[c, CONV_PAD + tm - CONV_HIST:CONV_PAD + tm, :]

    for c in range(POOL_SLABS):
        a_ext[c, 0:POOL_PAD, :] = a_ext[c, tm:tm + POOL_PAD, :]
    for c in range(CONV_SLABS):
        g_ext[c, 0:CONV_PAD, :] = g_ext[c, tm:tm + CONV_PAD, :]


def _mixer(x, kt, vb, w, tm):
    nb, seq, _ = x.shape
    nj = seq // tm
    n_tiles = nb * nj
    x2d = x.reshape(nb * seq, D_MODEL)
    cur = lambda s: jnp.minimum(s, n_tiles - 1)
    prev = lambda s: jnp.maximum(s - 1, 0)
    row_cur = pl.BlockSpec((tm, D_MODEL), lambda s: (cur(s), 0))
    row_prev = pl.BlockSpec((tm, D_MODEL), lambda s: (prev(s), 0))
    per_b = lambda shape, tile: pl.BlockSpec(shape, lambda s: (tile(s) // nj, 0, 0))
    consts = [w["g_mix"], w["w_in"], w["wmap"], w["b_map"], w["p_scale"], w["w_dw"], w["b_dw"],
              w["ln_g"], w["ln_b"], w["w_out"], w["g_attn"], w["w_q"], w["w_o"]]
    return pl.pallas_call(
        functools.partial(_mixer_kernel, tm=tm, nj=nj, n_tiles=n_tiles),
        grid=(n_tiles + 1,),
        in_specs=[row_cur, row_prev, per_b((1, D_MODEL, N_MEM), prev), per_b((1, N_MEM, D_MODEL), prev)]
                 + [_const_spec(c.shape) for c in consts],
        out_specs=[row_prev, per_b((1, POOL_HIST, POOL_WIDTH), cur),
                   per_b((1, CONV_HIST, CONV_CH), cur)],
        out_shape=[jax.ShapeDtypeStruct((nb * seq, D_MODEL), F32),
                   jax.ShapeDtypeStruct((nb, POOL_HIST, POOL_WIDTH), F32),
                   jax.ShapeDtypeStruct((nb, CONV_HIST, CONV_CH), F32)],
        scratch_shapes=[pltpu.VMEM((POOL_SLABS, POOL_PAD + tm, LANES), F32),
                        pltpu.VMEM((CONV_SLABS, CONV_PAD + tm, LANES), F32),
                        pltpu.VMEM((tm, CONV_CH), F32),
                        pltpu.VMEM((tm, D_MODEL), BF16)],
        compiler_params=pltpu.CompilerParams(dimension_semantics=("arbitrary",),
                                             vmem_limit_bytes=VMEM_LIMIT),
        name="mixer",
    )(x2d, x2d, kt, vb, *consts)


def _sample_pre_kernel(x_ref, sp_ref, sc_ref, gmix_ref, win_ref, wmap_ref, bmap_ref, pscale_ref,
                       wdw_ref, bdw_ref, lng_ref, lnb_ref, wout_ref, gattn_ref, wq_ref,
                       x1_ref, q_ref, pst_ref, cst_ref):
    x = x_ref[...]
    h = _rms(x, gmix_ref[...]).astype(BF16)
    u = _dot(h, win_ref[...])
    a = u[:, :POOL_WIDTH]
    glu = u[:, POOL_WIDTH:POOL_WIDTH + CONV_CH] * jax.nn.sigmoid(u[:, POOL_WIDTH + CONV_CH:])

    ds = []
    for g, w in enumerate(POOL_WINDOWS):
        c0 = g * POOL_GROUP_W
        a_g = a[:, c0:c0 + POOL_GROUP_W]
        win = a_g
        for i in range(1, w):
            win = win + sp_ref[POOL_HIST - i, :, c0:c0 + POOL_GROUP_W]
        ds.append(win * (1.0 / w) - a_g)
    ya = _pool_map(jnp.concatenate(ds, axis=-1), wmap_ref, bmap_ref[...], pscale_ref[...])

    conv = glu * wdw_ref[CONV_HIST, 0:1, :] + bdw_ref[...]
    for k in range(CONV_HIST):
        conv = conv + sc_ref[k] * wdw_ref[k, 0:1, :]
    yb = _layernorm_silu(conv, lng_ref[...], lnb_ref[...])

    pst_ref[0:POOL_HIST - 1] = sp_ref[1:POOL_HIST]
    pst_ref[POOL_HIST - 1] = a
    cst_ref[0:CONV_HIST - 1] = sc_ref[1:CONV_HIST]
    cst_ref[CONV_HIST - 1] = glu

    mix = jnp.concatenate([ya, yb], axis=-1).astype(BF16)
    x1 = x + _dot(mix, wout_ref[...])
    x1_ref[...] = x1
    q_ref[...] = _dot(_rms(x1, gattn_ref[...]).astype(BF16), wq_ref[...]) * Q_SCALE


def _sample_pre(x, sp, sc, w):
    n = x.shape[0]
    consts = [w["g_mix"], w["w_in"], w["wmap"], w["b_map"], w["p_scale"], w["w_dw"], w["b_dw"],
              w["ln_g"], w["ln_b"], w["w_out"], w["g_attn"], w["w_q"]]
    args = [x, sp, sc] + consts
    return pl.pallas_call(
        _sample_pre_kernel,
        grid=(1,),
        in_specs=[_const_spec(t.shape) for t in args],
        out_specs=[_const_spec((n, D_MODEL)), _const_spec((n, D_MODEL)),
                   _const_spec((POOL_HIST, n, POOL_WIDTH)), _const_spec((CONV_HIST, n, CONV_CH))],
        out_shape=[jax.ShapeDtypeStruct((n, D_MODEL), F32), jax.ShapeDtypeStruct((n, D_MODEL), F32),
                   jax.ShapeDtypeStruct((POOL_HIST, n, POOL_WIDTH), F32),
                   jax.ShapeDtypeStruct((CONV_HIST, n, CONV_CH), F32)],
        compiler_params=pltpu.CompilerParams(dimension_semantics=("arbitrary",),
                                             vmem_limit_bytes=VMEM_LIMIT),
        name="sample_pre",
    )(*args)


def _attend_one(q, k, v):
    kq = k * q[None]
    s = jnp.sum(kq + pltpu.roll(kq, MEM_HEADS, axis=1), axis=-1, keepdims=True)
    e = jnp.exp2(s - jnp.max(s, axis=0, keepdims=True))
    return jnp.sum(e * v, axis=0) / jnp.sum(e, axis=0)


def _ffn_rows(x2, gffn_ref, wg_ref, wu_ref, wd_ref, gfin_ref, y_ref):
    h = _rms(x2, gffn_ref[...]).astype(BF16)
    act = (_silu(_dot(h, wg_ref[...])) * _dot(h, wu_ref[...])).astype(BF16)
    x3 = x2 + _dot(act, wd_ref[...])
    y_ref[...] = _rms(x3, gfin_ref[...])


def _post_kernel(x1_ref, o_ref, wo_ref, gffn_ref, wg_ref, wu_ref, wd_ref, gfin_ref, y_ref):
    x2 = x1_ref[...] + _dot(o_ref[...].astype(BF16), wo_ref[...])
    _ffn_rows(x2, gffn_ref, wg_ref, wu_ref, wd_ref, gfin_ref, y_ref)


def _ffn_attn_kernel(x2_ref, gffn_ref, wg_ref, wu_ref, wd_ref, gfin_ref, q_ref, k_ref, v_ref,
                     y_ref, os_ref, *, sb):
    for i in range(sb):
        os_ref[i] = _attend_one(q_ref[i], k_ref[i], v_ref[i])
    _ffn_rows(x2_ref[...], gffn_ref, wg_ref, wu_ref, wd_ref, gfin_ref, y_ref)


def _to_head_rows(t):
    lead = t.shape[:-2]
    t = t.reshape(lead + (MEM_HEADS, 2, LANES))
    return jnp.swapaxes(t, -3, -2).reshape(lead + (HEAD_ROWS, LANES))


def _from_head_rows(t):
    lead = t.shape[:-2]
    t = t.reshape(lead + (2, MEM_HEADS, LANES))
    return jnp.swapaxes(t, -3, -2).reshape(lead + (MEM_HEADS, MEM_HEAD_DIM))


def _post_sample(x1, o, w):
    rows = x1.shape[0]
    args = [x1, o, w["w_o"], w["g_ffn"], w["w_gate"], w["w_up"], w["w_down"], w["g_final"]]
    return pl.pallas_call(
        _post_kernel, grid=(1,), in_specs=[_const_spec(a.shape) for a in args],
        out_specs=_const_spec((rows, D_MODEL)), out_shape=jax.ShapeDtypeStruct((rows, D_MODEL), F32),
        compiler_params=pltpu.CompilerParams(dimension_semantics=("arbitrary",), vmem_limit_bytes=VMEM_LIMIT),
        name="post")(*args)


def _ffn_attn(x2, w, tm, q, k, v):
    rows = x2.shape[0]
    steps = rows // tm
    n = q.shape[0]
    sb = n // steps
    assert sb * steps == n
    row = pl.BlockSpec((tm, D_MODEL), lambda i: (i, 0))
    consts = [w["g_ffn"], w["w_gate"], w["w_up"], w["w_down"], w["g_final"]]
    blk = pl.BlockSpec((sb, N_MEM, HEAD_ROWS, LANES), lambda i: (i, 0, 0, 0))
    vec = pl.BlockSpec((sb, HEAD_ROWS, LANES), lambda i: (i, 0, 0))
    y, os_ = pl.pallas_call(
        functools.partial(_ffn_attn_kernel, sb=sb),
        grid=(steps,),
        in_specs=[row] + [_const_spec(c.shape) for c in consts] + [vec, blk, blk],
        out_specs=[row, vec],
        out_shape=[jax.ShapeDtypeStruct((rows, D_MODEL), F32), jax.ShapeDtypeStruct((n, HEAD_ROWS, LANES), F32)],
        compiler_params=pltpu.CompilerParams(dimension_semantics=("arbitrary",), vmem_limit_bytes=VMEM_LIMIT),
        name="ffn_attn",
    )(x2, *consts, _to_head_rows(q.reshape(n, MEM_HEADS, MEM_HEAD_DIM)), _to_head_rows(k), _to_head_rows(v))
    return y, _from_head_rows(os_).reshape(n, D_MODEL)


def _prep_weights(g_mix, w_in, pool_map_w, pool_map_b, pool_scale, conv_dw_w, conv_dw_b, conv_ln_g,
                  conv_ln_b, w_out, g_attn, g_mem, w_q, w_k, w_v, w_o, g_ffn, w_gate, w_up, w_down,
                  g_final, l):
    vec = lambda v: v.reshape(1, -1)
    pm = pool_map_w[l]
    z = jnp.zeros((POOL_GROUP_W, POOL_GROUP_W), F32)
    wmap = jnp.stack([jnp.block([[pm[0], z], [z, pm[1]]]), jnp.block([[pm[2], z], [z, pm[3]]])])
    return dict(
        g_mix=vec(g_mix[l]), w_in=w_in[l].astype(BF16), wmap=wmap.astype(BF16), b_map=vec(pool_map_b[l]),
        p_scale=vec(pool_scale[l]),
        w_dw=jnp.broadcast_to(conv_dw_w[l][:, None, :], (CONV_K, SUBLANES, CONV_CH)),
        b_dw=vec(conv_dw_b[l]), ln_g=vec(conv_ln_g[l]), ln_b=vec(conv_ln_b[l]),
        w_out=w_out[l].astype(BF16), g_attn=vec(g_attn[l]), g_mem=vec(g_mem[l]),
        w_q=w_q[l].astype(BF16), w_k=w_k[l].astype(BF16), w_v=w_v[l].astype(BF16),
        w_o=w_o[l].astype(BF16), g_ffn=vec(g_ffn[l]), w_gate=w_gate[l].astype(BF16),
        w_up=w_up[l].astype(BF16), w_down=w_down[l].astype(BF16), g_final=vec(g_final))


def kernel(x_prompt, x_sample, mem_prompt, state_pool, state_conv, cache_mem_k, cache_mem_v, g_mix, w_in,
           pool_map_w, pool_map_b, pool_scale, conv_dw_w, conv_dw_b, conv_ln_g, conv_ln_b, w_out, g_attn,
           g_mem, w_q, w_k, w_v, w_o, g_ffn, w_gate, w_up, w_down, g_final):
    assert state_pool.shape[0] == 1, "single-layer trunk"
    nb, seq, _ = x_prompt.shape
    ns = x_sample.shape[0]
    w = _prep_weights(g_mix, w_in, pool_map_w, pool_map_b, pool_scale, conv_dw_w, conv_dw_b, conv_ln_g,
                      conv_ln_b, w_out, g_attn, g_mem, w_q, w_k, w_v, w_o, g_ffn, w_gate, w_up, w_down,
                      g_final, 0)

    x1s, qs, pool_s, conv_s = _sample_pre(x_sample.reshape(ns, D_MODEL), jnp.swapaxes(state_pool[0], 0, 1),
                                          jnp.swapaxes(state_conv[0], 0, 1), w)
    pool_s = jnp.swapaxes(pool_s, 0, 1)
    conv_s = jnp.swapaxes(conv_s, 0, 1)

    mk, mv, kt, vb = _mem_kv(mem_prompt, w["g_mem"], w["w_k"], w["w_v"])
    x2p, pool_p, conv_p = _mixer(x_prompt, kt, vb, w, tm=512)
    yp, os_ = _ffn_attn(x2p, w, 512, qs, cache_mem_k[0], cache_mem_v[0])
    ys = _post_sample(x1s, os_, w)

    return (yp.reshape(nb, seq, D_MODEL), ys.reshape(ns, 1, D_MODEL),
            pool_p[None], pool_s[None], conv_p[None], conv_s[None],
            _from_head_rows(mk.reshape(nb, N_MEM, HEAD_ROWS, LANES))[None],
            _from_head_rows(mv.reshape(nb, N_MEM, HEAD_ROWS, LANES))[None])
```

```python
import functools
import math

import jax
import jax.numpy as jnp
from jax import lax
from jax.experimental import pallas as pl
from jax.experimental.pallas import tpu as pltpu

D_MODEL = 1024
POOL_WINDOWS = (2, 4, 8, 16)
POOL_GROUP_W = 128
POOL_WIDTH = 512
POOL_HIST = 15
CONV_CH = 512
CONV_K = 31
CONV_HIST = 30
IN_COLS = POOL_WIDTH + 2 * CONV_CH
N_MEM = 256
MEM_HEADS = 4
MEM_HEAD_DIM = 256
EPS = 1e-6
ATTN_SCALE = 1.0 / math.sqrt(MEM_HEAD_DIM)
Q_SCALE = ATTN_SCALE * math.log2(math.e)

SUBLANES = 8
LANES = 128
POOL_SLABS = POOL_WIDTH // LANES
CONV_SLABS = CONV_CH // LANES
HEAD_ROWS = 2 * MEM_HEADS
CONV_ROWS = 64
IN_ROWS = 256
POOL_PAD = 16
CONV_PAD = 32
VMEM_LIMIT = 60 * 1024 * 1024

BF16 = jnp.bfloat16
F32 = jnp.float32


def _rms(x, g):
    ms = jnp.mean(x * x, axis=-1, keepdims=True)
    return x * lax.rsqrt(ms + EPS) * g


def _dot(a, b):
    return jnp.dot(a, b, preferred_element_type=F32)


def _silu(x):
    return x * jax.nn.sigmoid(x)


def _layernorm_silu(y, g, b):
    mu = jnp.mean(y, axis=-1, keepdims=True)
    yc = y - mu
    var = jnp.mean(yc * yc, axis=-1, keepdims=True)
    return _silu(yc * lax.rsqrt(var + EPS) * g + b)


def _pool_map(d, wmap_ref, bmap, pscale):
    db = d.astype(BF16)
    y = jnp.concatenate([_dot(db[:, :256], wmap_ref[0]), _dot(db[:, 256:], wmap_ref[1])], axis=-1)
    return (y + bmap) * pscale


def _const_spec(shape):
    nd = len(shape)
    return pl.BlockSpec(shape, lambda *_: (0,) * nd, pipeline_mode=pl.Buffered(1))


def _mem_kv_kernel(mem_ref, g_ref, wk_ref, wv_ref, k_ref, v_ref, kt_ref, vb_ref):
    m = _rms(mem_ref[0], g_ref[...]).astype(BF16)
    k = _dot(m, wk_ref[...])
    v = _dot(m, wv_ref[...])
    for r in range(HEAD_ROWS):
        half, head = divmod(r, MEM_HEADS)
        c0 = head * MEM_HEAD_DIM + half * LANES
        k_ref[0, pl.ds(r, N_MEM, stride=HEAD_ROWS), :] = k[:, c0:c0 + LANES]
        v_ref[0, pl.ds(r, N_MEM, stride=HEAD_ROWS), :] = v[:, c0:c0 + LANES]
    kt_ref[0] = k.T.astype(BF16)
    vb_ref[0] = v.astype(BF16)


def _mem_kv(mem, g_mem, wk, wv):
    nb = mem.shape[0]
    blk = lambda shape: pl.BlockSpec(shape, lambda b: (b, 0, 0))
    rows_blk = pl.BlockSpec((1, N_MEM * HEAD_ROWS, LANES), lambda b: (b, 0, 0))
    return pl.pallas_call(
        _mem_kv_kernel,
        grid=(nb,),
        in_specs=[blk((1, N_MEM, D_MODEL)), _const_spec((1, D_MODEL)),
                  _const_spec((D_MODEL, D_MODEL)), _const_spec((D_MODEL, D_MODEL))],
        out_specs=[rows_blk, rows_blk, blk((1, D_MODEL, N_MEM)), blk((1, N_MEM, D_MODEL))],
        out_shape=[jax.ShapeDtypeStruct((nb, N_MEM * HEAD_ROWS, LANES), F32),
                   jax.ShapeDtypeStruct((nb, N_MEM * HEAD_ROWS, LANES), F32),
                   jax.ShapeDtypeStruct((nb, D_MODEL, N_MEM), BF16),
                   jax.ShapeDtypeStruct((nb, N_MEM, D_MODEL), BF16)],
        compiler_params=pltpu.CompilerParams(dimension_semantics=("arbitrary",),
                                             vmem_limit_bytes=VMEM_LIMIT),
        name="mem_kv",
    )(mem, g_mem, wk, wv)


def _mixer_kernel(x_ref, xres_ref, kt_ref, vb_ref, gmix_ref, win_ref, wmap_ref, bmap_ref, pscale_ref,
                  wdw_ref, bdw_ref, lng_ref, lnb_ref, wout_ref, gattn_ref, wq_ref, wo_ref,
                  x2_ref, pst_ref, cst_ref, a_ext, g_ext, conv_buf, mix_buf, *, tm, nj, n_tiles):
    s = pl.program_id(0)
    j = s % nj

    @pl.when(s == 0)
    def _():
        mix_buf[...] = jnp.zeros(mix_buf.shape, BF16)

    @pl.when(j == 0)
    def _():
        a_ext[:, 0:POOL_PAD, :] = jnp.zeros((POOL_SLABS, POOL_PAD, LANES), F32)
        g_ext[:, 0:CONV_PAD, :] = jnp.zeros((CONV_SLABS, CONV_PAD, LANES), F32)

    for r0 in range(0, tm, IN_ROWS):
        h = _rms(x_ref[r0:r0 + IN_ROWS, :], gmix_ref[...]).astype(BF16)
        u = _dot(h, win_ref[...])
        glu = u[:, POOL_WIDTH:POOL_WIDTH + CONV_CH] * jax.nn.sigmoid(u[:, POOL_WIDTH + CONV_CH:])
        for c in range(POOL_SLABS):
            a_ext[c, POOL_PAD + r0:POOL_PAD + r0 + IN_ROWS, :] = u[:, c * LANES:(c + 1) * LANES]
        for c in range(CONV_SLABS):
            g_ext[c, CONV_PAD + r0:CONV_PAD + r0 + IN_ROWS, :] = glu[:, c * LANES:(c + 1) * LANES]

    x1 = xres_ref[...] + _dot(mix_buf[...], wout_ref[...])
    q = (_dot(_rms(x1, gattn_ref[...]).astype(BF16), wq_ref[...]) * Q_SCALE).astype(BF16)
    heads = []
    for hd in range(MEM_HEADS):
        c0 = hd * MEM_HEAD_DIM
        sc = _dot(q[:, c0:c0 + MEM_HEAD_DIM], kt_ref[0, c0:c0 + MEM_HEAD_DIM, :])
        e = jnp.exp2(sc - jnp.max(sc, axis=-1, keepdims=True))
        p = (e * (1.0 / jnp.sum(e, axis=-1, keepdims=True))).astype(BF16)
        heads.append(_dot(p, vb_ref[0, :, c0:c0 + MEM_HEAD_DIM]).astype(BF16))
    x2_ref[...] = x1 + _dot(jnp.concatenate(heads, axis=-1), wo_ref[...])

    pos = j * tm + lax.broadcasted_iota(jnp.int32, (tm, 1), 0)
    ds = []
    for g, w in enumerate(POOL_WINDOWS):
        a_g = a_ext[g, POOL_PAD:POOL_PAD + tm, :]
        win = a_g
        for i in range(1, w):
            win = win + a_ext[g, POOL_PAD - i:POOL_PAD - i + tm, :]
        inv_cnt = 1.0 / jnp.minimum(w, pos + 1).astype(F32)
        ds.append(win * inv_cnt - a_g)
    ya = _pool_map(jnp.concatenate(ds, axis=-1), wmap_ref, bmap_ref[...], pscale_ref[...])

    for t0 in range(0, tm, CONV_ROWS):
        for c in range(CONV_SLABS):
            acc = None
            for k in range(CONV_K):
                r0 = CONV_PAD - CONV_HIST + t0 + k
                g = g_ext[c, r0:r0 + CONV_ROWS, :].reshape(CONV_ROWS // SUBLANES, SUBLANES, LANES)
                term = g * wdw_ref[k, :, c * LANES:(c + 1) * LANES][None]
                acc = term if acc is None else acc + term
            conv_buf[t0:t0 + CONV_ROWS, c * LANES:(c + 1) * LANES] = acc.reshape(CONV_ROWS, LANES)
    yb = _layernorm_silu(conv_buf[...] + bdw_ref[...], lng_ref[...], lnb_ref[...])
    mix_buf[...] = jnp.concatenate([ya, yb], axis=-1).astype(BF16)

    @pl.when(s < n_tiles)
    def _():
        for c in range(POOL_SLABS):
            pst_ref[0, :, c * LANES:(c + 1) * LANES] = a_ext[c, POOL_PAD + tm - POOL_HIST:POOL_PAD + tm, :]
        for c in range(CONV_SLABS):
            cst_ref[0, :, c * LANES:(c + 1) * LANES] = g_ext[c, CONV_PAD + tm - CONV_HIST:CONV_PAD + tm, :]

    for c in range(POOL_SLABS):
        a_ext[c, 0:POOL_PAD, :] = a_ext[c, tm:tm + POOL_PAD, :]
    for c in range(CONV_SLABS):
        g_ext[c, 0:CONV_PAD, :] = g_ext[c, tm:tm + CONV_PAD, :]


def _mixer(x, kt, vb, w, tm):
    nb, seq, _ = x.shape
    nj = seq // tm
    n_tiles = nb * nj
    x2d = x.reshape(nb * seq, D_MODEL)
    cur = lambda s: jnp.minimum(s, n_tiles - 1)
    prev = lambda s: jnp.maximum(s - 1, 0)
    row_cur = pl.BlockSpec((tm, D_MODEL), lambda s: (cur(s), 0))
    row_prev = pl.BlockSpec((tm, D_MODEL), lambda s: (prev(s), 0))
    per_b = lambda shape, tile: pl.BlockSpec(shape, lambda s: (tile(s) // nj, 0, 0))
    consts = [w["g_mix"], w["w_in"], w["wmap"], w["b_map"], w["p_scale"], w["w_dw"], w["b_dw"],
              w["ln_g"], w["ln_b"], w["w_out"], w["g_attn"], w["w_q"], w["w_o"]]
    return pl.pallas_call(
        functools.partial(_mixer_kernel, tm=tm, nj=nj, n_tiles=n_tiles),
        grid=(n_tiles + 1,),
        in_specs=[row_cur, row_prev, per_b((1, D_MODEL, N_MEM), prev), per_b((1, N_MEM, D_MODEL), prev)]
                 + [_const_spec(c.shape) for c in consts],
        out_specs=[row_prev, per_b((1, POOL_HIST, POOL_WIDTH), cur),
                   per_b((1, CONV_HIST, CONV_CH), cur)],
        out_shape=[jax.ShapeDtypeStruct((nb * seq, D_MODEL), F32),
                   jax.ShapeDtypeStruct((nb, POOL_HIST, POOL_WIDTH), F32),
                   jax.ShapeDtypeStruct((nb, CONV_HIST, CONV_CH), F32)],
        scratch_shapes=[pltpu.VMEM((POOL_SLABS, POOL_PAD + tm, LANES), F32),
                        pltpu.VMEM((CONV_SLABS, CONV_PAD + tm, LANES), F32),
                        pltpu.VMEM((tm, CONV_CH), F32),
                        pltpu.VMEM((tm, D_MODEL), BF16)],
        compiler_params=pltpu.CompilerParams(dimension_semantics=("arbitrary",),
                                             vmem_limit_bytes=VMEM_LIMIT),
        name="mixer",
    )(x2d, x2d, kt, vb, *consts)


def _sample_pre_kernel(x_ref, sp_ref, sc_ref, gmix_ref, win_ref, wmap_ref, bmap_ref, pscale_ref,
                       wdw_ref, bdw_ref, lng_ref, lnb_ref, wout_ref, gattn_ref, wq_ref,
                       x1_ref, q_ref, pst_ref, cst_ref):
    x = x_ref[...]
    h = _rms(x, gmix_ref[...]).astype(BF16)
    u = _dot(h, win_ref[...])
    a = u[:, :POOL_WIDTH]
    glu = u[:, POOL_WIDTH:POOL_WIDTH + CONV_CH] * jax.nn.sigmoid(u[:, POOL_WIDTH + CONV_CH:])

    ds = []
    for g, w in enumerate(POOL_WINDOWS):
        c0 = g * POOL_GROUP_W
        a_g = a[:, c0:c0 + POOL_GROUP_W]
        win = a_g
        for i in range(1, w):
            win = win + sp_ref[POOL_HIST - i, :, c0:c0 + POOL_GROUP_W]
        ds.append(win * (1.0 / w) - a_g)
    ya = _pool_map(jnp.concatenate(ds, axis=-1), wmap_ref, bmap_ref[...], pscale_ref[...])

    conv = glu * wdw_ref[CONV_HIST, 0:1, :] + bdw_ref[...]
    for k in range(CONV_HIST):
        conv = conv + sc_ref[k] * wdw_ref[k, 0:1, :]
    yb = _layernorm_silu(conv, lng_ref[...], lnb_ref[...])

    pst_ref[0:POOL_HIST - 1] = sp_ref[1:POOL_HIST]
    pst_ref[POOL_HIST - 1] = a
    cst_ref[0:CONV_HIST - 1] = sc_ref[1:CONV_HIST]
    cst_ref[CONV_HIST - 1] = glu

    mix = jnp.concatenate([ya, yb], axis=-1).astype(BF16)
    x1 = x + _dot(mix, wout_ref[...])
    x1_ref[...] = x1
    q_ref[...] = _dot(_rms(x1, gattn_ref[...]).astype(BF16), wq_ref[...]) * Q_SCALE


def _sample_pre(x, sp, sc, w):
    n = x.shape[0]
    consts = [w["g_mix"], w["w_in"], w["wmap"], w["b_map"], w["p_scale"], w["w_dw"], w["b_dw"],
              w["ln_g"], w["ln_b"], w["w_out"], w["g_attn"], w["w_q"]]
    args = [x, sp, sc] + consts
    return pl.pallas_call(
        _sample_pre_kernel,
        grid=(1,),
        in_specs=[_const_spec(t.shape) for t in args],
        out_specs=[_const_spec((n, D_MODEL)), _const_spec((n, D_MODEL)),
                   _const_spec((POOL_HIST, n, POOL_WIDTH)), _const_spec((CONV_HIST, n, CONV_CH))],
        out_shape=[jax.ShapeDtypeStruct((n, D_MODEL), F32), jax.ShapeDtypeStruct((n, D_MODEL), F32),
                   jax.ShapeDtypeStruct((POOL_HIST, n, POOL_WIDTH), F32),
                   jax.ShapeDtypeStruct((CONV_HIST, n, CONV_CH), F32)],
        compiler_params=pltpu.CompilerParams(dimension_semantics=("arbitrary",),
                                             vmem_limit_bytes=VMEM_LIMIT),
        name="sample_pre",
    )(*args)


def _attend_one(q, k, v):
    kq = k * q[None]
    s = jnp.sum(kq + pltpu.roll(kq, MEM_HEADS, axis=1), axis=-1, keepdims=True)
    e = jnp.exp2(s - jnp.max(s, axis=0, keepdims=True))
    return jnp.sum(e * v, axis=0) / jnp.sum(e, axis=0)


def _ffn_rows(x2, gffn_ref, wg_ref, wu_ref, wd_ref, gfin_ref, y_ref):
    h = _rms(x2, gffn_ref[...]).astype(BF16)
    act = (_silu(_dot(h, wg_ref[...])) * _dot(h, wu_ref[...])).astype(BF16)
    x3 = x2 + _dot(act, wd_ref[...])
    y_ref[...] = _rms(x3, gfin_ref[...])


def _post_kernel(x1_ref, o_ref, wo_ref, gffn_ref, wg_ref, wu_ref, wd_ref, gfin_ref, y_ref):
    x2 = x1_ref[...] + _dot(o_ref[...].astype(BF16), wo_ref[...])
    _ffn_rows(x2, gffn_ref, wg_ref, wu_ref, wd_ref, gfin_ref, y_ref)


def _ffn_attn_kernel(x2_ref, gffn_ref, wg_ref, wu_ref, wd_ref, gfin_ref, q_ref, k_ref, v_ref,
                     y_ref, os_ref, *, sb):
    for i in range(sb):
        os_ref[i] = _attend_one(q_ref[i], k_ref[i], v_ref[i])
    _ffn_rows(x2_ref[...], gffn_ref, wg_ref, wu_ref, wd_ref, gfin_ref, y_ref)


def _to_head_rows(t):
    lead = t.shape[:-2]
    t = t.reshape(lead + (MEM_HEADS, 2, LANES))
    return jnp.swapaxes(t, -3, -2).reshape(lead + (HEAD_ROWS, LANES))


def _from_head_rows(t):
    lead = t.shape[:-2]
    t = t.reshape(lead + (2, MEM_HEADS, LANES))
    return jnp.swapaxes(t, -3, -2).reshape(lead + (MEM_HEADS, MEM_HEAD_DIM))


def _post_sample(x1, o, w):
    rows = x1.shape[0]
    args = [x1, o, w["w_o"], w["g_ffn"], w["w_gate"], w["w_up"], w["w_down"], w["g_final"]]
    return pl.pallas_call(
        _post_kernel, grid=(1,), in_specs=[_const_spec(a.shape) for a in args],
        out_specs=_const_spec((rows, D_MODEL)), out_shape=jax.ShapeDtypeStruct((rows, D_MODEL), F32),
        compiler_params=pltpu.CompilerParams(dimension_semantics=("arbitrary",), vmem_limit_bytes=VMEM_LIMIT),
        name="post")(*args)


def _ffn_attn(x2, w, tm, q, k, v):
    rows = x2.shape[0]
    steps = rows // tm
    n = q.shape[0]
    sb = n // steps
    assert sb * steps == n
    row = pl.BlockSpec((tm, D_MODEL), lambda i: (i, 0))
    consts = [w["g_ffn"], w["w_gate"], w["w_up"], w["w_down"], w["g_final"]]
    blk = pl.BlockSpec((sb, N_MEM, HEAD_ROWS, LANES), lambda i: (i, 0, 0, 0))
    vec = pl.BlockSpec((sb, HEAD_ROWS, LANES), lambda i: (i, 0, 0))
    y, os_ = pl.pallas_call(
        functools.partial(_ffn_attn_kernel, sb=sb),
        grid=(steps,),
        in_specs=[row] + [_const_spec(c.shape) for c in consts] + [vec, blk, blk],
        out_specs=[row, vec],
        out_shape=[jax.ShapeDtypeStruct((rows, D_MODEL), F32), jax.ShapeDtypeStruct((n, HEAD_ROWS, LANES), F32)],
        compiler_params=pltpu.CompilerParams(dimension_semantics=("arbitrary",), vmem_limit_bytes=VMEM_LIMIT),
        name="ffn_attn",
    )(x2, *consts, _to_head_rows(q.reshape(n, MEM_HEADS, MEM_HEAD_DIM)), _to_head_rows(k), _to_head_rows(v))
    return y, _from_head_rows(os_).reshape(n, D_MODEL)


def _prep_weights(g_mix, w_in, pool_map_w, pool_map_b, pool_scale, conv_dw_w, conv_dw_b, conv_ln_g,
                  conv_ln_b, w_out, g_attn, g_mem, w_q, w_k, w_v, w_o, g_ffn, w_gate, w_up, w_down,
                  g_final, l):
    vec = lambda v: v.reshape(1, -1)
    pm = pool_map_w[l]
    z = jnp.zeros((POOL_GROUP_W, POOL_GROUP_W), F32)
    wmap = jnp.stack([jnp.block([[pm[0], z], [z, pm[1]]]), jnp.block([[pm[2], z], [z, pm[3]]])])
    return dict(
        g_mix=vec(g_mix[l]), w_in=w_in[l].astype(BF16), wmap=wmap.astype(BF16), b_map=vec(pool_map_b[l]),
        p_scale=vec(pool_scale[l]),
        w_dw=jnp.broadcast_to(conv_dw_w[l][:, None, :], (CONV_K, SUBLANES, CONV_CH)),
        b_dw=vec(conv_dw_b[l]), ln_g=vec(conv_ln_g[l]), ln_b=vec(conv_ln_b[l]),
        w_out=w_out[l].astype(BF16), g_attn=vec(g_attn[l]), g_mem=vec(g_mem[l]),
        w_q=w_q[l].astype(BF16), w_k=w_k[l].astype(BF16), w_v=w_v[l].astype(BF16),
        w_o=w_o[l].astype(BF16), g_ffn=vec(g_ffn[l]), w_gate=w_gate[l].astype(BF16),
        w_up=w_up[l].astype(BF16), w_down=w_down[l].astype(BF16), g_final=vec(g_final))


def kernel(x_prompt, x_sample, mem_prompt, state_pool, state_conv, cache_mem_k, cache_mem_v, g_mix, w_in,
           pool_map_w, pool_map_b, pool_scale, conv_dw_w, conv_dw_b, conv_ln_g, conv_ln_b, w_out, g_attn,
           g_mem, w_q, w_k, w_v, w_o, g_ffn, w_gate, w_up, w_down, g_final):
    assert state_pool.shape[0] == 1, "single-layer trunk"
    nb, seq, _ = x_prompt.shape
    ns = x_sample.shape[0]
    w = _prep_weights(g_mix, w_in, pool_map_w, pool_map_b, pool_scale, conv_dw_w, conv_dw_b, conv_ln_g,
                      conv_ln_b, w_out, g_attn, g_mem, w_q, w_k, w_v, w_o, g_ffn, w_gate, w_up, w_down,
                      g_final, 0)

    x1s, qs, pool_s, conv_s = _sample_pre(x_sample.reshape(ns, D_MODEL), jnp.swapaxes(state_pool[0], 0, 1),
                                          jnp.swapaxes(state_conv[0], 0, 1), w)
    pool_s = jnp.swapaxes(pool_s, 0, 1)
    conv_s = jnp.swapaxes(conv_s, 0, 1)

    mk, mv, kt, vb = _mem_kv(mem_prompt, w["g_mem"], w["w_k"], w["w_v"])
    x2p, pool_p, conv_p = _mixer(x_prompt, kt, vb, w, tm=1024)
    yp, os_ = _ffn_attn(x2p, w, 512, qs, cache_mem_k[0], cache_mem_v[0])
    ys = _post_sample(x1s, os_, w)

    return (yp.reshape(nb, seq, D_MODEL), ys.reshape(ns, 1, D_MODEL),
            pool_p[None], pool_s[None], conv_p[None], conv_s[None],
            _from_head_rows(mk.reshape(nb, N_MEM, HEAD_ROWS, LANES))[None],
            _from_head_rows(mv.reshape(nb, N_MEM, HEAD_ROWS, LANES))[None])
```

```python
import functools
import math

import jax
import jax.numpy as jnp
from jax import lax
from jax.experimental import pallas as pl
from jax.experimental.pallas import tpu as pltpu

D_MODEL = 1024
POOL_WINDOWS = (2, 4, 8, 16)
POOL_GROUP_W = 128
POOL_WIDTH = 512
POOL_HIST = 15
CONV_CH = 512
CONV_K = 31
CONV_HIST = 30
IN_COLS = POOL_WIDTH + 2 * CONV_CH
N_MEM = 256
MEM_HEADS = 4
MEM_HEAD_DIM = 256
EPS = 1e-6
ATTN_SCALE = 1.0 / math.sqrt(MEM_HEAD_DIM)
Q_SCALE = ATTN_SCALE * math.log2(math.e)

SUBLANES = 8
LANES = 128
POOL_SLABS = POOL_WIDTH // LANES
CONV_SLABS = CONV_CH // LANES
HEAD_ROWS = 2 * MEM_HEADS
CONV_ROWS = 64
IN_ROWS = 256
POOL_PAD = 16
CONV_PAD = 32
VMEM_LIMIT = 60 * 1024 * 1024

BF16 = jnp.bfloat16
F32 = jnp.float32


def _rms(x, g):
    ms = jnp.mean(x * x, axis=-1, keepdims=True)
    return x * lax.rsqrt(ms + EPS) * g


def _dot(a, b):
    return jnp.dot(a, b, preferred_element_type=F32)


def _silu(x):
    return x * jax.nn.sigmoid(x)


def _layernorm_silu(y, g, b):
    mu = jnp.mean(y, axis=-1, keepdims=True)
    yc = y - mu
    var = jnp.mean(yc * yc, axis=-1, keepdims=True)
    return _silu(yc * lax.rsqrt(var + EPS) * g + b)


def _pool_map(d, wmap_ref, bmap, pscale):
    db = d.astype(BF16)
    y = jnp.concatenate([_dot(db[:, :256], wmap_ref[0]), _dot(db[:, 256:], wmap_ref[1])], axis=-1)
    return (y + bmap) * pscale


def _const_spec(shape):
    nd = len(shape)
    return pl.BlockSpec(shape, lambda *_: (0,) * nd, pipeline_mode=pl.Buffered(1))


def _mem_kv_kernel(mem_ref, g_ref, wk_ref, wv_ref, k_ref, v_ref, kt_ref, vb_ref):
    m = _rms(mem_ref[0], g_ref[...]).astype(BF16)
    k = _dot(m, wk_ref[...])
    v = _dot(m, wv_ref[...])
    for r in range(HEAD_ROWS):
        half, head = divmod(r, MEM_HEADS)
        c0 = head * MEM_HEAD_DIM + half * LANES
        k_ref[0, pl.ds(r, N_MEM, stride=HEAD_ROWS), :] = k[:, c0:c0 + LANES]
        v_ref[0, pl.ds(r, N_MEM, stride=HEAD_ROWS), :] = v[:, c0:c0 + LANES]
    kt_ref[0] = k.T.astype(BF16)
    vb_ref[0] = v.astype(BF16)


def _mem_kv(mem, g_mem, wk, wv):
    nb = mem.shape[0]
    blk = lambda shape: pl.BlockSpec(shape, lambda b: (b, 0, 0))
    rows_blk = pl.BlockSpec((1, N_MEM * HEAD_ROWS, LANES), lambda b: (b, 0, 0))
    return pl.pallas_call(
        _mem_kv_kernel,
        grid=(nb,),
        in_specs=[blk((1, N_MEM, D_MODEL)), _const_spec((1, D_MODEL)),
                  _const_spec((D_MODEL, D_MODEL)), _const_spec((D_MODEL, D_MODEL))],
        out_specs=[rows_blk, rows_blk, blk((1, D_MODEL, N_MEM)), blk((1, N_MEM, D_MODEL))],
        out_shape=[jax.ShapeDtypeStruct((nb, N_MEM * HEAD_ROWS, LANES), F32),
                   jax.ShapeDtypeStruct((nb, N_MEM * HEAD_ROWS, LANES), F32),
                   jax.ShapeDtypeStruct((nb, D_MODEL, N_MEM), BF16),
                   jax.ShapeDtypeStruct((nb, N_MEM, D_MODEL), BF16)],
        compiler_params=pltpu.CompilerParams(dimension_semantics=("arbitrary",),
                                             vmem_limit_bytes=VMEM_LIMIT),
        name="mem_kv",
    )(mem, g_mem, wk, wv)


def _mixer_kernel(x_ref, xres_ref, kt_ref, vb_ref, gmix_ref, win_ref, wmap_ref, bmap_ref, pscale_ref,
                  wdw_ref, bdw_ref, lng_ref, lnb_ref, wout_ref, gattn_ref, wq_ref, wo_ref,
                  wg32_ref, wu32_ref, wd32_ref,
                  x2_ref, pst_ref, cst_ref, wg16_ref, wu16_ref, wd16_ref,
                  a_ext, g_ext, conv_buf, mix_buf, *, tm, nj, n_tiles):
    s = pl.program_id(0)
    j = s % nj

    @pl.when(s == 0)
    def _():
        mix_buf[...] = jnp.zeros(mix_buf.shape, BF16)

    @pl.when(j == 0)
    def _():
        a_ext[:, 0:POOL_PAD, :] = jnp.zeros((POOL_SLABS, POOL_PAD, LANES), F32)
        g_ext[:, 0:CONV_PAD, :] = jnp.zeros((CONV_SLABS, CONV_PAD, LANES), F32)

    def project_and_attend():
        x1 = xres_ref[...] + _dot(mix_buf[...], wout_ref[...])
        q = (_dot(_rms(x1, gattn_ref[...]).astype(BF16), wq_ref[...]) * Q_SCALE).astype(BF16)
        heads = []
        for hd in range(MEM_HEADS):
            c0 = hd * MEM_HEAD_DIM
            sc = _dot(q[:, c0:c0 + MEM_HEAD_DIM], kt_ref[0, c0:c0 + MEM_HEAD_DIM, :])
            e = jnp.exp2(sc - jnp.max(sc, axis=-1, keepdims=True))
            p = (e * (1.0 / jnp.sum(e, axis=-1, keepdims=True))).astype(BF16)
            heads.append(_dot(p, vb_ref[0, :, c0:c0 + MEM_HEAD_DIM]).astype(BF16))
        x2_ref[...] = x1 + _dot(jnp.concatenate(heads, axis=-1), wo_ref[...])

    @pl.when(s < n_tiles)
    def _():
        wg16_ref[...] = wg32_ref[...].astype(BF16)
        wu16_ref[...] = wu32_ref[...].astype(BF16)
        wd16_ref[...] = wd32_ref[...].astype(BF16)

        for r0 in range(0, tm, IN_ROWS):
            h = _rms(x_ref[r0:r0 + IN_ROWS, :], gmix_ref[...]).astype(BF16)
            u = _dot(h, win_ref[...])
            glu = u[:, POOL_WIDTH:POOL_WIDTH + CONV_CH] * jax.nn.sigmoid(u[:, POOL_WIDTH + CONV_CH:])
            for c in range(POOL_SLABS):
                a_ext[c, POOL_PAD + r0:POOL_PAD + r0 + IN_ROWS, :] = u[:, c * LANES:(c + 1) * LANES]
            for c in range(CONV_SLABS):
                g_ext[c, CONV_PAD + r0:CONV_PAD + r0 + IN_ROWS, :] = glu[:, c * LANES:(c + 1) * LANES]

        project_and_attend()

        pos = j * tm + lax.broadcasted_iota(jnp.int32, (tm, 1), 0)
        ds = []
        for g, w in enumerate(POOL_WINDOWS):
            a_g = a_ext[g, POOL_PAD:POOL_PAD + tm, :]
            win = a_g
            for i in range(1, w):
                win = win + a_ext[g, POOL_PAD - i:POOL_PAD - i + tm, :]
            inv_cnt = 1.0 / jnp.minimum(w, pos + 1).astype(F32)
            ds.append(win * inv_cnt - a_g)
        ya = _pool_map(jnp.concatenate(ds, axis=-1), wmap_ref, bmap_ref[...], pscale_ref[...])

        for t0 in range(0, tm, CONV_ROWS):
            for c in range(CONV_SLABS):
                acc = None
                for k in range(CONV_K):
                    r0 = CONV_PAD - CONV_HIST + t0 + k
                    g = g_ext[c, r0:r0 + CONV_ROWS, :].reshape(CONV_ROWS // SUBLANES, SUBLANES, LANES)
                    term = g * wdw_ref[k, :, c * LANES:(c + 1) * LANES][None]
                    acc = term if acc is None else acc + term
                conv_buf[t0:t0 + CONV_ROWS, c * LANES:(c + 1) * LANES] = acc.reshape(CONV_ROWS, LANES)
        yb = _layernorm_silu(conv_buf[...] + bdw_ref[...], lng_ref[...], lnb_ref[...])
        mix_buf[...] = jnp.concatenate([ya, yb], axis=-1).astype(BF16)

        for c in range(POOL_SLABS):
            pst_ref[0, :, c * LANES:(c + 1) * LANES] = a_ext[c, POOL_PAD + tm - POOL_HIST:POOL_PAD + tm, :]
            a_ext[c, 0:POOL_PAD, :] = a_ext[c, tm:tm + POOL_PAD, :]
        for c in range(CONV_SLABS):
            cst_ref[0, :, c * LANES:(c + 1) * LANES] = g_ext[c, CONV_PAD + tm - CONV_HIST:CONV_PAD + tm, :]
            g_ext[c, 0:CONV_PAD, :] = g_ext[c, tm:tm + CONV_PAD, :]

    @pl.when(s == n_tiles)
    def _():
        project_and_attend()


def _mixer(x, kt, vb, w, ffn_w, tm):
    nb, seq, _ = x.shape
    nj = seq // tm
    n_tiles = nb * nj
    x2d = x.reshape(nb * seq, D_MODEL)
    cur = lambda s: jnp.minimum(s, n_tiles - 1)
    prev = lambda s: jnp.maximum(s - 1, 0)
    row_cur = pl.BlockSpec((tm, D_MODEL), lambda s: (cur(s), 0))
    row_prev = pl.BlockSpec((tm, D_MODEL), lambda s: (prev(s), 0))
    per_b = lambda shape, tile: pl.BlockSpec(shape, lambda s: (tile(s) // nj, 0, 0))
    consts = [w["g_mix"], w["w_in"], w["wmap"], w["b_map"], w["p_scale"], w["w_dw"], w["b_dw"],
              w["ln_g"], w["ln_b"], w["w_out"], w["g_attn"], w["w_q"], w["w_o"]]
    ffn32 = [ffn_w["w_gate"], ffn_w["w_up"], ffn_w["w_down"]]
    ffn_specs = [pl.BlockSpec((t.shape[0] // n_tiles, t.shape[1]), lambda s: (cur(s), 0)) for t in ffn32]
    return pl.pallas_call(
        functools.partial(_mixer_kernel, tm=tm, nj=nj, n_tiles=n_tiles),
        grid=(n_tiles + 1,),
        in_specs=[row_cur, row_prev, per_b((1, D_MODEL, N_MEM), prev), per_b((1, N_MEM, D_MODEL), prev)]
                 + [_const_spec(c.shape) for c in consts] + ffn_specs,
        out_specs=[row_prev, per_b((1, POOL_HIST, POOL_WIDTH), cur),
                   per_b((1, CONV_HIST, CONV_CH), cur)] + ffn_specs,
        out_shape=[jax.ShapeDtypeStruct((nb * seq, D_MODEL), F32),
                   jax.ShapeDtypeStruct((nb, POOL_HIST, POOL_WIDTH), F32),
                   jax.ShapeDtypeStruct((nb, CONV_HIST, CONV_CH), F32)]
                  + [jax.ShapeDtypeStruct(t.shape, BF16) for t in ffn32],
        scratch_shapes=[pltpu.VMEM((POOL_SLABS, POOL_PAD + tm, LANES), F32),
                        pltpu.VMEM((CONV_SLABS, CONV_PAD + tm, LANES), F32),
                        pltpu.VMEM((tm, CONV_CH), F32),
                        pltpu.VMEM((tm, D_MODEL), BF16)],
        compiler_params=pltpu.CompilerParams(dimension_semantics=("arbitrary",),
                                             vmem_limit_bytes=VMEM_LIMIT),
        name="mixer",
    )(x2d, x2d, kt, vb, *consts, *ffn32)


def _sample_pre_kernel(x_ref, sp_ref, sc_ref, gmix_ref, win_ref, wmap_ref, bmap_ref, pscale_ref,
                       wdw_ref, bdw_ref, lng_ref, lnb_ref, wout_ref, gattn_ref, wq_ref,
                       x1_ref, q_ref, pst_ref, cst_ref):
    x = x_ref[...]
    h = _rms(x, gmix_ref[...]).astype(BF16)
    u = _dot(h, win_ref[...])
    a = u[:, :POOL_WIDTH]
    glu = u[:, POOL_WIDTH:POOL_WIDTH + CONV_CH] * jax.nn.sigmoid(u[:, POOL_WIDTH + CONV_CH:])

    ds = []
    for g, w in enumerate(POOL_WINDOWS):
        c0 = g * POOL_GROUP_W
        a_g = a[:, c0:c0 + POOL_GROUP_W]
        win = a_g
        for i in range(1, w):
            win = win + sp_ref[POOL_HIST - i, :, c0:c0 + POOL_GROUP_W]
        ds.append(win * (1.0 / w) - a_g)
    ya = _pool_map(jnp.concatenate(ds, axis=-1), wmap_ref, bmap_ref[...], pscale_ref[...])

    conv = glu * wdw_ref[CONV_HIST, 0:1, :] + bdw_ref[...]
    for k in range(CONV_HIST):
        conv = conv + sc_ref[k] * wdw_ref[k, 0:1, :]
    yb = _layernorm_silu(conv, lng_ref[...], lnb_ref[...])

    pst_ref[0:POOL_HIST - 1] = sp_ref[1:POOL_HIST]
    pst_ref[POOL_HIST - 1] = a
    cst_ref[0:CONV_HIST - 1] = sc_ref[1:CONV_HIST]
    cst_ref[CONV_HIST - 1] = glu

    mix = jnp.concatenate([ya, yb], axis=-1).astype(BF16)
    x1 = x + _dot(mix, wout_ref[...])
    x1_ref[...] = x1
    q_ref[...] = _dot(_rms(x1, gattn_ref[...]).astype(BF16), wq_ref[...]) * Q_SCALE


def _sample_pre(x, sp, sc, w):
    n = x.shape[0]
    consts = [w["g_mix"], w["w_in"], w["wmap"], w["b_map"], w["p_scale"], w["w_dw"], w["b_dw"],
              w["ln_g"], w["ln_b"], w["w_out"], w["g_attn"], w["w_q"]]
    args = [x, sp, sc] + consts
    return pl.pallas_call(
        _sample_pre_kernel,
        grid=(1,),
        in_specs=[_const_spec(t.shape) for t in args],
        out_specs=[_const_spec((n, D_MODEL)), _const_spec((n, D_MODEL)),
                   _const_spec((POOL_HIST, n, POOL_WIDTH)), _const_spec((CONV_HIST, n, CONV_CH))],
        out_shape=[jax.ShapeDtypeStruct((n, D_MODEL), F32), jax.ShapeDtypeStruct((n, D_MODEL), F32),
                   jax.ShapeDtypeStruct((POOL_HIST, n, POOL_WIDTH), F32),
                   jax.ShapeDtypeStruct((CONV_HIST, n, CONV_CH), F32)],
        compiler_params=pltpu.CompilerParams(dimension_semantics=("arbitrary",),
                                             vmem_limit_bytes=VMEM_LIMIT),
        name="sample_pre",
    )(*args)


def _attend_one(q, k, v):
    kq = k * q[None]
    s = jnp.sum(kq + pltpu.roll(kq, MEM_HEADS, axis=1), axis=-1, keepdims=True)
    e = jnp.exp2(s - jnp.max(s, axis=0, keepdims=True))
    return jnp.sum(e * v, axis=0) / jnp.sum(e, axis=0)


def _ffn_rows(x2, gffn_ref, wg_ref, wu_ref, wd_ref, gfin_ref, y_ref):
    h = _rms(x2, gffn_ref[...]).astype(BF16)
    act = (_silu(_dot(h, wg_ref[...])) * _dot(h, wu_ref[...])).astype(BF16)
    x3 = x2 + _dot(act, wd_ref[...])
    y_ref[...] = _rms(x3, gfin_ref[...])


def _post_kernel(x1_ref, o_ref, wo_ref, gffn_ref, wg_ref, wu_ref, wd_ref, gfin_ref, y_ref):
    x2 = x1_ref[...] + _dot(o_ref[...].astype(BF16), wo_ref[...])
    _ffn_rows(x2, gffn_ref, wg_ref, wu_ref, wd_ref, gfin_ref, y_ref)


def _ffn_attn_kernel(x2_ref, gffn_ref, wg_ref, wu_ref, wd_ref, gfin_ref, q_ref, k_ref, v_ref,
                     y_ref, os_ref, *, sb):
    for i in range(sb):
        os_ref[i] = _attend_one(q_ref[i], k_ref[i], v_ref[i])
    _ffn_rows(x2_ref[...], gffn_ref, wg_ref, wu_ref, wd_ref, gfin_ref, y_ref)


def _to_head_rows(t):
    lead = t.shape[:-2]
    t = t.reshape(lead + (MEM_HEADS, 2, LANES))
    return jnp.swapaxes(t, -3, -2).reshape(lead + (HEAD_ROWS, LANES))


def _from_head_rows(t):
    lead = t.shape[:-2]
    t = t.reshape(lead + (2, MEM_HEADS, LANES))
    return jnp.swapaxes(t, -3, -2).reshape(lead + (MEM_HEADS, MEM_HEAD_DIM))


def _post_sample(x1, o, w):
    rows = x1.shape[0]
    args = [x1, o, w["w_o"], w["g_ffn"], w["w_gate"], w["w_up"], w["w_down"], w["g_final"]]
    return pl.pallas_call(
        _post_kernel, grid=(1,), in_specs=[_const_spec(a.shape) for a in args],
        out_specs=_const_spec((rows, D_MODEL)), out_shape=jax.ShapeDtypeStruct((rows, D_MODEL), F32),
        compiler_params=pltpu.CompilerParams(dimension_semantics=("arbitrary",), vmem_limit_bytes=VMEM_LIMIT),
        name="post")(*args)


def _ffn_attn(x2, w, tm, q, k, v):
    rows = x2.shape[0]
    steps = rows // tm
    n = q.shape[0]
    sb = n // steps
    assert sb * steps == n
    row = pl.BlockSpec((tm, D_MODEL), lambda i: (i, 0))
    consts = [w["g_ffn"], w["w_gate"], w["w_up"], w["w_down"], w["g_final"]]
    blk = pl.BlockSpec((sb, N_MEM, HEAD_ROWS, LANES), lambda i: (i, 0, 0, 0))
    vec = pl.BlockSpec((sb, HEAD_ROWS, LANES), lambda i: (i, 0, 0))
    y, os_ = pl.pallas_call(
        functools.partial(_ffn_attn_kernel, sb=sb),
        grid=(steps,),
        in_specs=[row] + [_const_spec(c.shape) for c in consts] + [vec, blk, blk],
        out_specs=[row, vec],
        out_shape=[jax.ShapeDtypeStruct((rows, D_MODEL), F32), jax.ShapeDtypeStruct((n, HEAD_ROWS, LANES), F32)],
        compiler_params=pltpu.CompilerParams(dimension_semantics=("arbitrary",), vmem_limit_bytes=VMEM_LIMIT),
        name="ffn_attn",
    )(x2, *consts, _to_head_rows(q.reshape(n, MEM_HEADS, MEM_HEAD_DIM)), _to_head_rows(k), _to_head_rows(v))
    return y, _from_head_rows(os_).reshape(n, D_MODEL)


def _prep_weights(g_mix, w_in, pool_map_w, pool_map_b, pool_scale, conv_dw_w, conv_dw_b, conv_ln_g,
                  conv_ln_b, w_out, g_attn, g_mem, w_q, w_k, w_v, w_o, g_ffn, w_gate, w_up, w_down,
                  g_final, l):
    vec = lambda v: v.reshape(1, -1)
    pm = pool_map_w[l]
    z = jnp.zeros((POOL_GROUP_W, POOL_GROUP_W), F32)
    wmap = jnp.stack([jnp.block([[pm[0], z], [z, pm[1]]]), jnp.block([[pm[2], z], [z, pm[3]]])])
    return dict(
        g_mix=vec(g_mix[l]), w_in=w_in[l].astype(BF16), wmap=wmap.astype(BF16), b_map=vec(pool_map_b[l]),
        p_scale=vec(pool_scale[l]),
        w_dw=jnp.broadcast_to(conv_dw_w[l][:, None, :], (CONV_K, SUBLANES, CONV_CH)),
        b_dw=vec(conv_dw_b[l]), ln_g=vec(conv_ln_g[l]), ln_b=vec(conv_ln_b[l]),
        w_out=w_out[l].astype(BF16), g_attn=vec(g_attn[l]), g_mem=vec(g_mem[l]),
        w_q=w_q[l].astype(BF16), w_k=w_k[l].astype(BF16), w_v=w_v[l].astype(BF16),
        w_o=w_o[l].astype(BF16), g_ffn=vec(g_ffn[l]), g_final=vec(g_final))


def kernel(x_prompt, x_sample, mem_prompt, state_pool, state_conv, cache_mem_k, cache_mem_v, g_mix, w_in,
           pool_map_w, pool_map_b, pool_scale, conv_dw_w, conv_dw_b, conv_ln_g, conv_ln_b, w_out, g_attn,
           g_mem, w_q, w_k, w_v, w_o, g_ffn, w_gate, w_up, w_down, g_final):
    assert state_pool.shape[0] == 1, "single-layer trunk"
    nb, seq, _ = x_prompt.shape
    ns = x_sample.shape[0]
    w = _prep_weights(g_mix, w_in, pool_map_w, pool_map_b, pool_scale, conv_dw_w, conv_dw_b, conv_ln_g,
                      conv_ln_b, w_out, g_attn, g_mem, w_q, w_k, w_v, w_o, g_ffn, w_gate, w_up, w_down,
                      g_final, 0)

    x1s, qs, pool_s, conv_s = _sample_pre(x_sample.reshape(ns, D_MODEL), jnp.swapaxes(state_pool[0], 0, 1),
                                          jnp.swapaxes(state_conv[0], 0, 1), w)
    pool_s = jnp.swapaxes(pool_s, 0, 1)
    conv_s = jnp.swapaxes(conv_s, 0, 1)

    mk, mv, kt, vb = _mem_kv(mem_prompt, w["g_mem"], w["w_k"], w["w_v"])
    ffn_f32 = dict(w_gate=w_gate[0], w_up=w_up[0], w_down=w_down[0])
    x2p, pool_p, conv_p, w["w_gate"], w["w_up"], w["w_down"] = _mixer(x_prompt, kt, vb, w, ffn_f32, tm=1024)
    yp, os_ = _ffn_attn(x2p, w, 512, qs, cache_mem_k[0], cache_mem_v[0])
    ys = _post_sample(x1s, os_, w)

    return (yp.reshape(nb, seq, D_MODEL), ys.reshape(ns, 1, D_MODEL),
            pool_p[None], pool_s[None], conv_p[None], conv_s[None],
            _from_head_rows(mk.reshape(nb, N_MEM, HEAD_ROWS, LANES))[None],
            _from_head_rows(mv.reshape(nb, N_MEM, HEAD_ROWS, LANES))[None])
```

```python
import functools
import math

import jax
import jax.numpy as jnp
from jax import lax
from jax.experimental import pallas as pl
from jax.experimental.pallas import tpu as pltpu

D_MODEL = 1024
POOL_WINDOWS = (2, 4, 8, 16)
POOL_GROUP_W = 128
POOL_WIDTH = 512
POOL_HIST = 15
CONV_CH = 512
CONV_K = 31
CONV_HIST = 30
IN_COLS = POOL_WIDTH + 2 * CONV_CH
N_MEM = 256
MEM_HEADS = 4
MEM_HEAD_DIM = 256
EPS = 1e-6
ATTN_SCALE = 1.0 / math.sqrt(MEM_HEAD_DIM)
Q_SCALE = ATTN_SCALE * math.log2(math.e)

SUBLANES = 8
LANES = 128
POOL_SLABS = POOL_WIDTH // LANES
CONV_SLABS = CONV_CH // LANES
HEAD_ROWS = 2 * MEM_HEADS
CONV_ROWS = 64
IN_ROWS = 256
POOL_PAD = 16
CONV_PAD = 32
VMEM_LIMIT = 60 * 1024 * 1024

BF16 = jnp.bfloat16
F32 = jnp.float32


def _rms(x, g):
    ms = jnp.mean(x * x, axis=-1, keepdims=True)
    return x * lax.rsqrt(ms + EPS) * g


def _dot(a, b):
    return jnp.dot(a, b, preferred_element_type=F32)


def _silu(x):
    return x * jax.nn.sigmoid(x)


def _layernorm_silu(y, g, b):
    mu = jnp.mean(y, axis=-1, keepdims=True)
    yc = y - mu
    var = jnp.mean(yc * yc, axis=-1, keepdims=True)
    return _silu(yc * lax.rsqrt(var + EPS) * g + b)


def _pool_map(d, wmap_ref, bmap, pscale):
    db = d.astype(BF16)
    y = jnp.concatenate([_dot(db[:, :256], wmap_ref[0]), _dot(db[:, 256:], wmap_ref[1])], axis=-1)
    return (y + bmap) * pscale


def _load_row_major(ref, n):
    return jnp.concatenate([ref[pl.ds(j, n, stride=SUBLANES), :] for j in range(D_MODEL // LANES)], axis=-1)


def _store_row_major(ref, val):
    for j in range(D_MODEL // LANES):
        ref[pl.ds(j, val.shape[0], stride=SUBLANES), :] = val[:, j * LANES:(j + 1) * LANES]


def _head_row_cols(r):
    half, head = divmod(r, MEM_HEADS)
    return head * MEM_HEAD_DIM + half * LANES


def _const_spec(shape):
    nd = len(shape)
    return pl.BlockSpec(shape, lambda *_: (0,) * nd, pipeline_mode=pl.Buffered(1))


def _mem_kv_kernel(mem_ref, g_ref, wk_ref, wv_ref, k_ref, v_ref, kt_ref, vb_ref):
    m = _rms(mem_ref[0], g_ref[...]).astype(BF16)
    k = _dot(m, wk_ref[...])
    v = _dot(m, wv_ref[...])
    for r in range(HEAD_ROWS):
        c0 = _head_row_cols(r)
        k_ref[0, pl.ds(r, N_MEM, stride=HEAD_ROWS), :] = k[:, c0:c0 + LANES]
        v_ref[0, pl.ds(r, N_MEM, stride=HEAD_ROWS), :] = v[:, c0:c0 + LANES]
    kt_ref[0] = k.T.astype(BF16)
    vb_ref[0] = v.astype(BF16)


def _mem_kv(mem, g_mem, wk, wv):
    nb = mem.shape[0]
    blk = lambda shape: pl.BlockSpec(shape, lambda b: (b, 0, 0))
    rows_blk = pl.BlockSpec((1, N_MEM * HEAD_ROWS, LANES), lambda b: (b, 0, 0))
    return pl.pallas_call(
        _mem_kv_kernel,
        grid=(nb,),
        in_specs=[blk((1, N_MEM, D_MODEL)), _const_spec((1, D_MODEL)),
                  _const_spec((D_MODEL, D_MODEL)), _const_spec((D_MODEL, D_MODEL))],
        out_specs=[rows_blk, rows_blk, blk((1, D_MODEL, N_MEM)), blk((1, N_MEM, D_MODEL))],
        out_shape=[jax.ShapeDtypeStruct((nb, N_MEM * HEAD_ROWS, LANES), F32),
                   jax.ShapeDtypeStruct((nb, N_MEM * HEAD_ROWS, LANES), F32),
                   jax.ShapeDtypeStruct((nb, D_MODEL, N_MEM), BF16),
                   jax.ShapeDtypeStruct((nb, N_MEM, D_MODEL), BF16)],
        compiler_params=pltpu.CompilerParams(dimension_semantics=("arbitrary",),
                                             vmem_limit_bytes=VMEM_LIMIT),
        name="mem_kv",
    )(mem, g_mem, wk, wv)


def _mixer_kernel(x_ref, xres_ref, kt_ref, vb_ref, gmix_ref, win_ref, wmap_ref, bmap_ref, pscale_ref,
                  wdw_ref, bdw_ref, lng_ref, lnb_ref, wout_ref, gattn_ref, wq_ref, wo_ref,
                  wg32_ref, wu32_ref, wd32_ref,
                  x2_ref, pst_ref, cst_ref, wg16_ref, wu16_ref, wd16_ref,
                  a_ext, g_ext, conv_buf, mix_buf, *, tm, nj, n_tiles):
    s = pl.program_id(0)
    j = s % nj

    @pl.when(s == 0)
    def _():
        mix_buf[...] = jnp.zeros(mix_buf.shape, BF16)

    @pl.when(j == 0)
    def _():
        a_ext[:, 0:POOL_PAD, :] = jnp.zeros((POOL_SLABS, POOL_PAD, LANES), F32)
        g_ext[:, 0:CONV_PAD, :] = jnp.zeros((CONV_SLABS, CONV_PAD, LANES), F32)

    def project_and_attend():
        x1 = xres_ref[...] + _dot(mix_buf[...], wout_ref[...])
        q = (_dot(_rms(x1, gattn_ref[...]).astype(BF16), wq_ref[...]) * Q_SCALE).astype(BF16)
        heads = []
        for hd in range(MEM_HEADS):
            c0 = hd * MEM_HEAD_DIM
            sc = _dot(q[:, c0:c0 + MEM_HEAD_DIM], kt_ref[0, c0:c0 + MEM_HEAD_DIM, :])
            e = jnp.exp2(sc - jnp.max(sc, axis=-1, keepdims=True))
            p = (e * (1.0 / jnp.sum(e, axis=-1, keepdims=True))).astype(BF16)
            heads.append(_dot(p, vb_ref[0, :, c0:c0 + MEM_HEAD_DIM]).astype(BF16))
        x2_ref[...] = x1 + _dot(jnp.concatenate(heads, axis=-1), wo_ref[...])

    @pl.when(s < n_tiles)
    def _():
        wg16_ref[...] = wg32_ref[...].astype(BF16)
        wu16_ref[...] = wu32_ref[...].astype(BF16)
        wd16_ref[...] = wd32_ref[...].astype(BF16)

        for r0 in range(0, tm, IN_ROWS):
            h = _rms(x_ref[r0:r0 + IN_ROWS, :], gmix_ref[...]).astype(BF16)
            u = _dot(h, win_ref[...])
            glu = u[:, POOL_WIDTH:POOL_WIDTH + CONV_CH] * jax.nn.sigmoid(u[:, POOL_WIDTH + CONV_CH:])
            for c in range(POOL_SLABS):
                a_ext[c, POOL_PAD + r0:POOL_PAD + r0 + IN_ROWS, :] = u[:, c * LANES:(c + 1) * LANES]
            for c in range(CONV_SLABS):
                g_ext[c, CONV_PAD + r0:CONV_PAD + r0 + IN_ROWS, :] = glu[:, c * LANES:(c + 1) * LANES]

        project_and_attend()

        pos = j * tm + lax.broadcasted_iota(jnp.int32, (tm, 1), 0)
        ds = []
        for g, w in enumerate(POOL_WINDOWS):
            a_g = a_ext[g, POOL_PAD:POOL_PAD + tm, :]
            win = a_g
            for i in range(1, w):
                win = win + a_ext[g, POOL_PAD - i:POOL_PAD - i + tm, :]
            inv_cnt = 1.0 / jnp.minimum(w, pos + 1).astype(F32)
            ds.append(win * inv_cnt - a_g)
        ya = _pool_map(jnp.concatenate(ds, axis=-1), wmap_ref, bmap_ref[...], pscale_ref[...])

        for t0 in range(0, tm, CONV_ROWS):
            for c in range(CONV_SLABS):
                acc = None
                for k in range(CONV_K):
                    r0 = CONV_PAD - CONV_HIST + t0 + k
                    g = g_ext[c, r0:r0 + CONV_ROWS, :].reshape(CONV_ROWS // SUBLANES, SUBLANES, LANES)
                    term = g * wdw_ref[k, :, c * LANES:(c + 1) * LANES][None]
                    acc = term if acc is None else acc + term
                conv_buf[t0:t0 + CONV_ROWS, c * LANES:(c + 1) * LANES] = acc.reshape(CONV_ROWS, LANES)
        yb = _layernorm_silu(conv_buf[...] + bdw_ref[...], lng_ref[...], lnb_ref[...])
        mix_buf[...] = jnp.concatenate([ya, yb], axis=-1).astype(BF16)

        for c in range(POOL_SLABS):
            pst_ref[0, :, c * LANES:(c + 1) * LANES] = a_ext[c, POOL_PAD + tm - POOL_HIST:POOL_PAD + tm, :]
            a_ext[c, 0:POOL_PAD, :] = a_ext[c, tm:tm + POOL_PAD, :]
        for c in range(CONV_SLABS):
            cst_ref[0, :, c * LANES:(c + 1) * LANES] = g_ext[c, CONV_PAD + tm - CONV_HIST:CONV_PAD + tm, :]
            g_ext[c, 0:CONV_PAD, :] = g_ext[c, tm:tm + CONV_PAD, :]

    @pl.when(s == n_tiles)
    def _():
        project_and_attend()


def _mixer(x, kt, vb, w, ffn_w, tm):
    nb, seq, _ = x.shape
    nj = seq // tm
    n_tiles = nb * nj
    x2d = x.reshape(nb * seq, D_MODEL)
    cur = lambda s: jnp.minimum(s, n_tiles - 1)
    prev = lambda s: jnp.maximum(s - 1, 0)
    row_cur = pl.BlockSpec((tm, D_MODEL), lambda s: (cur(s), 0))
    row_prev = pl.BlockSpec((tm, D_MODEL), lambda s: (prev(s), 0))
    per_b = lambda shape, tile: pl.BlockSpec(shape, lambda s: (tile(s) // nj, 0, 0))
    consts = [w["g_mix"], w["w_in"], w["wmap"], w["b_map"], w["p_scale"], w["w_dw"], w["b_dw"],
              w["ln_g"], w["ln_b"], w["w_out"], w["g_attn"], w["w_q"], w["w_o"]]
    ffn32 = [ffn_w["w_gate"], ffn_w["w_up"], ffn_w["w_down"]]
    ffn_specs = [pl.BlockSpec((t.shape[0] // n_tiles, t.shape[1]), lambda s: (cur(s), 0)) for t in ffn32]
    return pl.pallas_call(
        functools.partial(_mixer_kernel, tm=tm, nj=nj, n_tiles=n_tiles),
        grid=(n_tiles + 1,),
        in_specs=[row_cur, row_prev, per_b((1, D_MODEL, N_MEM), prev), per_b((1, N_MEM, D_MODEL), prev)]
                 + [_const_spec(c.shape) for c in consts] + ffn_specs,
        out_specs=[row_prev, per_b((1, POOL_HIST, POOL_WIDTH), cur),
                   per_b((1, CONV_HIST, CONV_CH), cur)] + ffn_specs,
        out_shape=[jax.ShapeDtypeStruct((nb * seq, D_MODEL), F32),
                   jax.ShapeDtypeStruct((nb, POOL_HIST, POOL_WIDTH), F32),
                   jax.ShapeDtypeStruct((nb, CONV_HIST, CONV_CH), F32)]
                  + [jax.ShapeDtypeStruct(t.shape, BF16) for t in ffn32],
        scratch_shapes=[pltpu.VMEM((POOL_SLABS, POOL_PAD + tm, LANES), F32),
                        pltpu.VMEM((CONV_SLABS, CONV_PAD + tm, LANES), F32),
                        pltpu.VMEM((tm, CONV_CH), F32),
                        pltpu.VMEM((tm, D_MODEL), BF16)],
        compiler_params=pltpu.CompilerParams(dimension_semantics=("arbitrary",),
                                             vmem_limit_bytes=VMEM_LIMIT),
        name="mixer",
    )(x2d, x2d, kt, vb, *consts, *ffn32)


def _sample_pre_kernel(x_ref, sp_ref, sc_ref, gmix_ref, win_ref, wmap_ref, bmap_ref, pscale_ref,
                       wdw_ref, bdw_ref, lng_ref, lnb_ref, wout_ref, gattn_ref, wq_ref,
                       x1_ref, q_ref, pst_ref, cst_ref):
    n = x1_ref.shape[0]
    x = _load_row_major(x_ref, n)
    h = _rms(x, gmix_ref[...]).astype(BF16)
    u = _dot(h, win_ref[...])
    a = u[:, :POOL_WIDTH]
    glu = u[:, POOL_WIDTH:POOL_WIDTH + CONV_CH] * jax.nn.sigmoid(u[:, POOL_WIDTH + CONV_CH:])

    ds = []
    for g, w in enumerate(POOL_WINDOWS):
        c0 = g * POOL_GROUP_W
        a_g = a[:, c0:c0 + POOL_GROUP_W]
        win = a_g
        for i in range(1, w):
            win = win + sp_ref[POOL_HIST - i, :, c0:c0 + POOL_GROUP_W]
        ds.append(win * (1.0 / w) - a_g)
    ya = _pool_map(jnp.concatenate(ds, axis=-1), wmap_ref, bmap_ref[...], pscale_ref[...])

    conv = glu * wdw_ref[CONV_HIST, 0:1, :] + bdw_ref[...]
    for k in range(CONV_HIST):
        conv = conv + sc_ref[k] * wdw_ref[k, 0:1, :]
    yb = _layernorm_silu(conv, lng_ref[...], lnb_ref[...])

    pst_ref[0:POOL_HIST - 1] = sp_ref[1:POOL_HIST]
    pst_ref[POOL_HIST - 1] = a
    cst_ref[0:CONV_HIST - 1] = sc_ref[1:CONV_HIST]
    cst_ref[CONV_HIST - 1] = glu

    mix = jnp.concatenate([ya, yb], axis=-1).astype(BF16)
    x1 = x + _dot(mix, wout_ref[...])
    x1_ref[...] = x1
    q = _dot(_rms(x1, gattn_ref[...]).astype(BF16), wq_ref[...]) * Q_SCALE
    for r in range(HEAD_ROWS):
        c0 = _head_row_cols(r)
        q_ref[pl.ds(r, n, stride=HEAD_ROWS), :] = q[:, c0:c0 + LANES]


def _sample_pre(x, sp, sc, w):
    n = sp.shape[1]
    consts = [w["g_mix"], w["w_in"], w["wmap"], w["b_map"], w["p_scale"], w["w_dw"], w["b_dw"],
              w["ln_g"], w["ln_b"], w["w_out"], w["g_attn"], w["w_q"]]
    args = [x, sp, sc] + consts
    return pl.pallas_call(
        _sample_pre_kernel,
        grid=(1,),
        in_specs=[_const_spec(t.shape) for t in args],
        out_specs=[_const_spec((n, D_MODEL)), _const_spec((n * HEAD_ROWS, LANES)),
                   _const_spec((POOL_HIST, n, POOL_WIDTH)), _const_spec((CONV_HIST, n, CONV_CH))],
        out_shape=[jax.ShapeDtypeStruct((n, D_MODEL), F32), jax.ShapeDtypeStruct((n * HEAD_ROWS, LANES), F32),
                   jax.ShapeDtypeStruct((POOL_HIST, n, POOL_WIDTH), F32),
                   jax.ShapeDtypeStruct((CONV_HIST, n, CONV_CH), F32)],
        compiler_params=pltpu.CompilerParams(dimension_semantics=("arbitrary",),
                                             vmem_limit_bytes=VMEM_LIMIT),
        name="sample_pre",
    )(*args)


def _attend_one(q, k, v):
    kq = k * q[None]
    s = jnp.sum(kq + pltpu.roll(kq, MEM_HEADS, axis=1), axis=-1, keepdims=True)
    e = jnp.exp2(s - jnp.max(s, axis=0, keepdims=True))
    return jnp.sum(e * v, axis=0) / jnp.sum(e, axis=0)


def _ffn(x2, gffn_ref, wg_ref, wu_ref, wd_ref, gfin_ref):
    h = _rms(x2, gffn_ref[...]).astype(BF16)
    act = (_silu(_dot(h, wg_ref[...])) * _dot(h, wu_ref[...])).astype(BF16)
    x3 = x2 + _dot(act, wd_ref[...])
    return _rms(x3, gfin_ref[...])


def _ffn_attn_kernel(x2_ref, gffn_ref, wg_ref, wu_ref, wd_ref, gfin_ref, q_ref, k_ref, v_ref, x1s_ref, wo_ref,
                     y_ref, ys_ref, os_all, *, sb, steps):
    i = pl.program_id(0)
    for t in range(sb):
        row0 = pl.multiple_of((i * sb + t) * HEAD_ROWS, HEAD_ROWS)
        os_all[pl.ds(row0, HEAD_ROWS), :] = _attend_one(q_ref[t], k_ref[t], v_ref[t])
    y_ref[...] = _ffn(x2_ref[...], gffn_ref, wg_ref, wu_ref, wd_ref, gfin_ref)

    @pl.when(i == steps - 1)
    def _():
        n = x1s_ref.shape[0]
        cols = {_head_row_cols(r): os_all[pl.ds(r, n, stride=HEAD_ROWS), :] for r in range(HEAD_ROWS)}
        o = jnp.concatenate([cols[c] for c in sorted(cols)], axis=-1).astype(BF16)
        x2s = x1s_ref[...] + _dot(o, wo_ref[...])
        _store_row_major(ys_ref, _ffn(x2s, gffn_ref, wg_ref, wu_ref, wd_ref, gfin_ref))


def _to_head_rows(t):
    lead = t.shape[:-2]
    t = t.reshape(lead + (MEM_HEADS, 2, LANES))
    return jnp.swapaxes(t, -3, -2).reshape(lead + (HEAD_ROWS, LANES))


def _from_head_rows(t):
    lead = t.shape[:-2]
    t = t.reshape(lead + (2, MEM_HEADS, LANES))
    return jnp.swapaxes(t, -3, -2).reshape(lead + (MEM_HEADS, MEM_HEAD_DIM))


def _ffn_attn(x2, w, tm, q8, k, v, x1s):
    rows = x2.shape[0]
    steps = rows // tm
    n = x1s.shape[0]
    sb = n // steps
    assert sb * steps == n
    row = pl.BlockSpec((tm, D_MODEL), lambda i: (i, 0))
    consts = [w["g_ffn"], w["w_gate"], w["w_up"], w["w_down"], w["g_final"]]
    blk = pl.BlockSpec((sb, N_MEM, HEAD_ROWS, LANES), lambda i: (i, 0, 0, 0))
    vec = pl.BlockSpec((sb, HEAD_ROWS, LANES), lambda i: (i, 0, 0))
    return pl.pallas_call(
        functools.partial(_ffn_attn_kernel, sb=sb, steps=steps),
        grid=(steps,),
        in_specs=[row] + [_const_spec(c.shape) for c in consts] + [vec, blk, blk]
                 + [_const_spec(x1s.shape), _const_spec(w["w_o"].shape)],
        out_specs=[row, _const_spec((n * SUBLANES, LANES))],
        out_shape=[jax.ShapeDtypeStruct((rows, D_MODEL), F32), jax.ShapeDtypeStruct((n * SUBLANES, LANES), F32)],
        scratch_shapes=[pltpu.VMEM((n * HEAD_ROWS, LANES), F32)],
        compiler_params=pltpu.CompilerParams(dimension_semantics=("arbitrary",), vmem_limit_bytes=VMEM_LIMIT),
        name="ffn_attn",
    )(x2, *consts, q8.reshape(n, HEAD_ROWS, LANES), _to_head_rows(k), _to_head_rows(v), x1s, w["w_o"])


def _prep_weights(g_mix, w_in, pool_map_w, pool_map_b, pool_scale, conv_dw_w, conv_dw_b, conv_ln_g,
                  conv_ln_b, w_out, g_attn, g_mem, w_q, w_k, w_v, w_o, g_ffn, w_gate, w_up, w_down,
                  g_final, l):
    vec = lambda v: v.reshape(1, -1)
    pm = pool_map_w[l]
    z = jnp.zeros((POOL_GROUP_W, POOL_GROUP_W), F32)
    wmap = jnp.stack([jnp.block([[pm[0], z], [z, pm[1]]]), jnp.block([[pm[2], z], [z, pm[3]]])])
    return dict(
        g_mix=vec(g_mix[l]), w_in=w_in[l].astype(BF16), wmap=wmap.astype(BF16), b_map=vec(pool_map_b[l]),
        p_scale=vec(pool_scale[l]),
        w_dw=jnp.broadcast_to(conv_dw_w[l][:, None, :], (CONV_K, SUBLANES, CONV_CH)),
        b_dw=vec(conv_dw_b[l]), ln_g=vec(conv_ln_g[l]), ln_b=vec(conv_ln_b[l]),
        w_out=w_out[l].astype(BF16), g_attn=vec(g_attn[l]), g_mem=vec(g_mem[l]),
        w_q=w_q[l].astype(BF16), w_k=w_k[l].astype(BF16), w_v=w_v[l].astype(BF16),
        w_o=w_o[l].astype(BF16), g_ffn=vec(g_ffn[l]), g_final=vec(g_final))


def kernel(x_prompt, x_sample, mem_prompt, state_pool, state_conv, cache_mem_k, cache_mem_v, g_mix, w_in,
           pool_map_w, pool_map_b, pool_scale, conv_dw_w, conv_dw_b, conv_ln_g, conv_ln_b, w_out, g_attn,
           g_mem, w_q, w_k, w_v, w_o, g_ffn, w_gate, w_up, w_down, g_final):
    assert state_pool.shape[0] == 1, "single-layer trunk"
    nb, seq, _ = x_prompt.shape
    ns = x_sample.shape[0]
    w = _prep_weights(g_mix, w_in, pool_map_w, pool_map_b, pool_scale, conv_dw_w, conv_dw_b, conv_ln_g,
                      conv_ln_b, w_out, g_attn, g_mem, w_q, w_k, w_v, w_o, g_ffn, w_gate, w_up, w_down,
                      g_final, 0)

    x1s, qs, pool_s, conv_s = _sample_pre(x_sample.reshape(ns * SUBLANES, LANES), jnp.swapaxes(state_pool[0], 0, 1),
                                          jnp.swapaxes(state_conv[0], 0, 1), w)
    pool_s = jnp.swapaxes(pool_s, 0, 1)
    conv_s = jnp.swapaxes(conv_s, 0, 1)

    mk, mv, kt, vb = _mem_kv(mem_prompt, w["g_mem"], w["w_k"], w["w_v"])
    ffn_f32 = dict(w_gate=w_gate[0], w_up=w_up[0], w_down=w_down[0])
    x2p, pool_p, conv_p, w["w_gate"], w["w_up"], w["w_down"] = _mixer(x_prompt, kt, vb, w, ffn_f32, tm=1024)
    yp, ys = _ffn_attn(x2p, w, 512, qs, cache_mem_k[0], cache_mem_v[0], x1s)

    return (yp.reshape(nb, seq, D_MODEL), ys.reshape(ns, 1, D_MODEL),
            pool_p[None], pool_s[None], conv_p[None], conv_s[None],
            _from_head_rows(mk.reshape(nb, N_MEM, HEAD_ROWS, LANES))[None],
            _from_head_rows(mv.reshape(nb, N_MEM, HEAD_ROWS, LANES))[None])
```

```python
import functools
import math

import jax
import jax.numpy as jnp
from jax import lax
from jax.experimental import pallas as pl
from jax.experimental.pallas import tpu as pltpu

D_MODEL = 1024
POOL_WINDOWS = (2, 4, 8, 16)
POOL_GROUP_W = 128
POOL_WIDTH = 512
POOL_HIST = 15
CONV_CH = 512
CONV_K = 31
CONV_HIST = 30
IN_COLS = POOL_WIDTH + 2 * CONV_CH
N_MEM = 256
MEM_HEADS = 4
MEM_HEAD_DIM = 256
EPS = 1e-6
ATTN_SCALE = 1.0 / math.sqrt(MEM_HEAD_DIM)
Q_SCALE = ATTN_SCALE * math.log2(math.e)

SUBLANES = 8
LANES = 128
POOL_SLABS = POOL_WIDTH // LANES
CONV_SLABS = CONV_CH // LANES
HEAD_ROWS = 2 * MEM_HEADS
CONV_ROWS = 64
POOL_PAD = 16
CONV_PAD = 32
VMEM_LIMIT = 60 * 1024 * 1024

BF16 = jnp.bfloat16
F32 = jnp.float32


def _rms(x, g):
    ms = jnp.mean(x * x, axis=-1, keepdims=True)
    return x * lax.rsqrt(ms + EPS) * g


def _dot(a, b):
    return jnp.dot(a, b, preferred_element_type=F32)


def _silu(x):
    return x * jax.nn.sigmoid(x)


def _layernorm_silu(y, g, b):
    mu = jnp.mean(y, axis=-1, keepdims=True)
    yc = y - mu
    var = jnp.mean(yc * yc, axis=-1, keepdims=True)
    return _silu(yc * lax.rsqrt(var + EPS) * g + b)


def _pool_map(d, wmap_ref, bmap, pscale):
    db = d.astype(BF16)
    y = jnp.concatenate([_dot(db[:, :256], wmap_ref[0]), _dot(db[:, 256:], wmap_ref[1])], axis=-1)
    return (y + bmap) * pscale


def _load_row_major(ref, n):
    return jnp.concatenate([ref[pl.ds(j, n, stride=SUBLANES), :] for j in range(D_MODEL // LANES)], axis=-1)


def _store_row_major(ref, val):
    for j in range(D_MODEL // LANES):
        ref[pl.ds(j, val.shape[0], stride=SUBLANES), :] = val[:, j * LANES:(j + 1) * LANES]


def _head_row_cols(r):
    half, head = divmod(r, MEM_HEADS)
    return head * MEM_HEAD_DIM + half * LANES


def _const_spec(shape):
    nd = len(shape)
    return pl.BlockSpec(shape, lambda *_: (0,) * nd, pipeline_mode=pl.Buffered(1))


def _mem_kv_kernel(mem_ref, g_ref, wk_ref, wv_ref, k_ref, v_ref, kt_ref, vb_ref):
    nbb = mem_ref.shape[0]
    m = _rms(mem_ref[...].reshape(nbb * N_MEM, D_MODEL), g_ref[...]).astype(BF16)
    k = _dot(m, wk_ref[...])
    v = _dot(m, wv_ref[...])
    for b in range(nbb):
        kb = k[b * N_MEM:(b + 1) * N_MEM]
        vb = v[b * N_MEM:(b + 1) * N_MEM]
        for r in range(HEAD_ROWS):
            c0 = _head_row_cols(r)
            k_ref[b, pl.ds(r, N_MEM, stride=HEAD_ROWS), :] = kb[:, c0:c0 + LANES]
            v_ref[b, pl.ds(r, N_MEM, stride=HEAD_ROWS), :] = vb[:, c0:c0 + LANES]
        kt_ref[b] = kb.T.astype(BF16)
        vb_ref[b] = vb.astype(BF16)


def _mem_kv(mem, g_mem, wk, wv, nbb=2):
    nb = mem.shape[0]
    blk = lambda shape: pl.BlockSpec(shape, lambda b: (b, 0, 0))
    rows_blk = blk((nbb, N_MEM * HEAD_ROWS, LANES))
    return pl.pallas_call(
        _mem_kv_kernel,
        grid=(nb // nbb,),
        in_specs=[blk((nbb, N_MEM, D_MODEL)), _const_spec((1, D_MODEL)),
                  _const_spec((D_MODEL, D_MODEL)), _const_spec((D_MODEL, D_MODEL))],
        out_specs=[rows_blk, rows_blk, blk((nbb, D_MODEL, N_MEM)), blk((nbb, N_MEM, D_MODEL))],
        out_shape=[jax.ShapeDtypeStruct((nb, N_MEM * HEAD_ROWS, LANES), F32),
                   jax.ShapeDtypeStruct((nb, N_MEM * HEAD_ROWS, LANES), F32),
                   jax.ShapeDtypeStruct((nb, D_MODEL, N_MEM), BF16),
                   jax.ShapeDtypeStruct((nb, N_MEM, D_MODEL), BF16)],
        compiler_params=pltpu.CompilerParams(dimension_semantics=("arbitrary",),
                                             vmem_limit_bytes=VMEM_LIMIT),
        name="mem_kv",
    )(mem, g_mem, wk, wv)


def _mixer_kernel(x_ref, xres_ref, kt_ref, vb_ref, gmix_ref, win_ref, wmap_ref, bmap_ref, pscale_ref,
                  wdw_ref, bdw_ref, lng_ref, lnb_ref, wout_ref, gattn_ref, wq_ref, wo_ref,
                  wg32_ref, wu32_ref, wd32_ref,
                  x2_ref, pst_ref, cst_ref, wg16_ref, wu16_ref, wd16_ref,
                  a_ext, g_ext, conv_buf, mix_buf, *, tm, nj, n_tiles):
    s = pl.program_id(0)
    j = s % nj

    @pl.when(s == 0)
    def _():
        mix_buf[...] = jnp.zeros(mix_buf.shape, BF16)

    @pl.when(j == 0)
    def _():
        a_ext[:, 0:POOL_PAD, :] = jnp.zeros((POOL_SLABS, POOL_PAD, LANES), F32)
        g_ext[:, 0:CONV_PAD, :] = jnp.zeros((CONV_SLABS, CONV_PAD, LANES), F32)

    def project_and_attend():
        x1 = xres_ref[...] + _dot(mix_buf[...], wout_ref[...])
        q = (_dot(_rms(x1, gattn_ref[...]).astype(BF16), wq_ref[...]) * Q_SCALE).astype(BF16)
        heads = []
        for hd in range(MEM_HEADS):
            c0 = hd * MEM_HEAD_DIM
            sc = _dot(q[:, c0:c0 + MEM_HEAD_DIM], kt_ref[0, c0:c0 + MEM_HEAD_DIM, :])
            e = jnp.exp2(sc - jnp.max(sc, axis=-1, keepdims=True))
            p = (e * (1.0 / jnp.sum(e, axis=-1, keepdims=True))).astype(BF16)
            heads.append(_dot(p, vb_ref[0, :, c0:c0 + MEM_HEAD_DIM]).astype(BF16))
        x2_ref[...] = x1 + _dot(jnp.concatenate(heads, axis=-1), wo_ref[...])

    @pl.when(s < n_tiles)
    def _():
        wg16_ref[...] = wg32_ref[...].astype(BF16)
        wu16_ref[...] = wu32_ref[...].astype(BF16)
        wd16_ref[...] = wd32_ref[...].astype(BF16)

        h = _rms(x_ref[...], gmix_ref[...]).astype(BF16)
        for c in range(CONV_SLABS):
            u = _dot(h, win_ref[:, 2 * c * LANES:2 * (c + 1) * LANES])
            g_ext[c, CONV_PAD:CONV_PAD + tm, :] = u[:, :LANES] * jax.nn.sigmoid(u[:, LANES:])
        u = _dot(h, win_ref[:, 2 * CONV_CH:])
        for c in range(POOL_SLABS):
            a_ext[c, POOL_PAD:POOL_PAD + tm, :] = u[:, c * LANES:(c + 1) * LANES]

        project_and_attend()

        pos = j * tm + lax.broadcasted_iota(jnp.int32, (tm, 1), 0)
        ds = []
        for g, w in enumerate(POOL_WINDOWS):
            a_g = a_ext[g, POOL_PAD:POOL_PAD + tm, :]
            win = a_g
            for i in range(1, w):
                win = win + a_ext[g, POOL_PAD - i:POOL_PAD - i + tm, :]
            inv_cnt = 1.0 / jnp.minimum(w, pos + 1).astype(F32)
            ds.append(win * inv_cnt - a_g)
        ya = _pool_map(jnp.concatenate(ds, axis=-1), wmap_ref, bmap_ref[...], pscale_ref[...])

        for c in range(CONV_SLABS):
            for t0 in range(0, tm, CONV_ROWS):
                acc = None
                for k in range(CONV_K):
                    r0 = CONV_PAD - CONV_HIST + t0 + k
                    g = g_ext[c, r0:r0 + CONV_ROWS, :].reshape(CONV_ROWS // SUBLANES, SUBLANES, LANES)
                    term = g * wdw_ref[k, :, c * LANES:(c + 1) * LANES][None]
                    acc = term if acc is None else acc + term
                conv_buf[t0:t0 + CONV_ROWS, c * LANES:(c + 1) * LANES] = acc.reshape(CONV_ROWS, LANES)
        yb = _layernorm_silu(conv_buf[...] + bdw_ref[...], lng_ref[...], lnb_ref[...])
        mix_buf[...] = jnp.concatenate([ya, yb], axis=-1).astype(BF16)

        for c in range(POOL_SLABS):
            pst_ref[0, :, c * LANES:(c + 1) * LANES] = a_ext[c, POOL_PAD + tm - POOL_HIST:POOL_PAD + tm, :]
            a_ext[c, 0:POOL_PAD, :] = a_ext[c, tm:tm + POOL_PAD, :]
        for c in range(CONV_SLABS):
            cst_ref[0, :, c * LANES:(c + 1) * LANES] = g_ext[c, CONV_PAD + tm - CONV_HIST:CONV_PAD + tm, :]
            g_ext[c, 0:CONV_PAD, :] = g_ext[c, tm:tm + CONV_PAD, :]

    @pl.when(s == n_tiles)
    def _():
        project_and_attend()


def _mixer(x, kt, vb, w, ffn_w, tm):
    nb, seq, _ = x.shape
    nj = seq // tm
    n_tiles = nb * nj
    x2d = x.reshape(nb * seq, D_MODEL)
    cur = lambda s: jnp.minimum(s, n_tiles - 1)
    prev = lambda s: jnp.maximum(s - 1, 0)
    row_cur = pl.BlockSpec((tm, D_MODEL), lambda s: (cur(s), 0))
    row_prev = pl.BlockSpec((tm, D_MODEL), lambda s: (prev(s), 0))
    per_b = lambda shape, tile: pl.BlockSpec(shape, lambda s: (tile(s) // nj, 0, 0))
    consts = [w["g_mix"], w["w_in"], w["wmap"], w["b_map"], w["p_scale"], w["w_dw"], w["b_dw"],
              w["ln_g"], w["ln_b"], w["w_out"], w["g_attn"], w["w_q"], w["w_o"]]
    ffn32 = [ffn_w["w_gate"], ffn_w["w_up"], ffn_w["w_down"]]
    ffn_specs = [pl.BlockSpec((t.shape[0] // n_tiles, t.shape[1]), lambda s: (cur(s), 0)) for t in ffn32]
    return pl.pallas_call(
        functools.partial(_mixer_kernel, tm=tm, nj=nj, n_tiles=n_tiles),
        grid=(n_tiles + 1,),
        in_specs=[row_cur, row_prev, per_b((1, D_MODEL, N_MEM), prev), per_b((1, N_MEM, D_MODEL), prev)]
                 + [_const_spec(c.shape) for c in consts] + ffn_specs,
        out_specs=[row_prev, per_b((1, POOL_HIST, POOL_WIDTH), cur),
                   per_b((1, CONV_HIST, CONV_CH), cur)] + ffn_specs,
        out_shape=[jax.ShapeDtypeStruct((nb * seq, D_MODEL), F32),
                   jax.ShapeDtypeStruct((nb, POOL_HIST, POOL_WIDTH), F32),
                   jax.ShapeDtypeStruct((nb, CONV_HIST, CONV_CH), F32)]
                  + [jax.ShapeDtypeStruct(t.shape, BF16) for t in ffn32],
        scratch_shapes=[pltpu.VMEM((POOL_SLABS, POOL_PAD + tm, LANES), F32),
                        pltpu.VMEM((CONV_SLABS, CONV_PAD + tm, LANES), F32),
                        pltpu.VMEM((tm, CONV_CH), F32),
                        pltpu.VMEM((tm, D_MODEL), BF16)],
        compiler_params=pltpu.CompilerParams(dimension_semantics=("arbitrary",),
                                             vmem_limit_bytes=VMEM_LIMIT),
        name="mixer",
    )(x2d, x2d, kt, vb, *consts, *ffn32)


def _sample_pre_kernel(x_ref, sp_ref, sc_ref, gmix_ref, win_ref, wmap_ref, bmap_ref, pscale_ref,
                       wdw_ref, bdw_ref, lng_ref, lnb_ref, wout_ref, gattn_ref, wq_ref,
                       x1_ref, q_ref, pst_ref, cst_ref):
    n = x1_ref.shape[0]
    x = _load_row_major(x_ref, n)
    h = _rms(x, gmix_ref[...]).astype(BF16)
    u = _dot(h, win_ref[...])
    a = u[:, 2 * CONV_CH:]
    glu = jnp.concatenate([u[:, 2 * c * LANES:(2 * c + 1) * LANES] * jax.nn.sigmoid(u[:, (2 * c + 1) * LANES:2 * (c + 1) * LANES])
                           for c in range(CONV_SLABS)], axis=-1)

    ds = []
    for g, w in enumerate(POOL_WINDOWS):
        c0 = g * POOL_GROUP_W
        a_g = a[:, c0:c0 + POOL_GROUP_W]
        win = a_g
        for i in range(1, w):
            win = win + sp_ref[POOL_HIST - i, :, c0:c0 + POOL_GROUP_W]
        ds.append(win * (1.0 / w) - a_g)
    ya = _pool_map(jnp.concatenate(ds, axis=-1), wmap_ref, bmap_ref[...], pscale_ref[...])

    conv = glu * wdw_ref[CONV_HIST, 0:1, :] + bdw_ref[...]
    for k in range(CONV_HIST):
        conv = conv + sc_ref[k] * wdw_ref[k, 0:1, :]
    yb = _layernorm_silu(conv, lng_ref[...], lnb_ref[...])

    pst_ref[0:POOL_HIST - 1] = sp_ref[1:POOL_HIST]
    pst_ref[POOL_HIST - 1] = a
    cst_ref[0:CONV_HIST - 1] = sc_ref[1:CONV_HIST]
    cst_ref[CONV_HIST - 1] = glu

    mix = jnp.concatenate([ya, yb], axis=-1).astype(BF16)
    x1 = x + _dot(mix, wout_ref[...])
    x1_ref[...] = x1
    q = _dot(_rms(x1, gattn_ref[...]).astype(BF16), wq_ref[...]) * Q_SCALE
    for r in range(HEAD_ROWS):
        c0 = _head_row_cols(r)
        q_ref[pl.ds(r, n, stride=HEAD_ROWS), :] = q[:, c0:c0 + LANES]


def _sample_pre(x, sp, sc, w):
    n = sp.shape[1]
    consts = [w["g_mix"], w["w_in"], w["wmap"], w["b_map"], w["p_scale"], w["w_dw"], w["b_dw"],
              w["ln_g"], w["ln_b"], w["w_out"], w["g_attn"], w["w_q"]]
    args = [x, sp, sc] + consts
    return pl.pallas_call(
        _sample_pre_kernel,
        grid=(1,),
        in_specs=[_const_spec(t.shape) for t in args],
        out_specs=[_const_spec((n, D_MODEL)), _const_spec((n * HEAD_ROWS, LANES)),
                   _const_spec((POOL_HIST, n, POOL_WIDTH)), _const_spec((CONV_HIST, n, CONV_CH))],
        out_shape=[jax.ShapeDtypeStruct((n, D_MODEL), F32), jax.ShapeDtypeStruct((n * HEAD_ROWS, LANES), F32),
                   jax.ShapeDtypeStruct((POOL_HIST, n, POOL_WIDTH), F32),
                   jax.ShapeDtypeStruct((CONV_HIST, n, CONV_CH), F32)],
        compiler_params=pltpu.CompilerParams(dimension_semantics=("arbitrary",),
                                             vmem_limit_bytes=VMEM_LIMIT),
        name="sample_pre",
    )(*args)


def _attend_one(q, k, v):
    kq = k * q[None]
    s = jnp.sum(kq + pltpu.roll(kq, MEM_HEADS, axis=1), axis=-1, keepdims=True)
    e = jnp.exp2(s - jnp.max(s, axis=0, keepdims=True))
    return jnp.sum(e * v, axis=0) / jnp.sum(e, axis=0)


def _ffn(x2, gffn_ref, wg_ref, wu_ref, wd_ref, gfin_ref):
    h = _rms(x2, gffn_ref[...]).astype(BF16)
    act = (_silu(_dot(h, wg_ref[...])) * _dot(h, wu_ref[...])).astype(BF16)
    x3 = x2 + _dot(act, wd_ref[...])
    return _rms(x3, gfin_ref[...])


def _ffn_attn_kernel(x2_ref, gffn_ref, wg_ref, wu_ref, wd_ref, gfin_ref, q_ref, k_ref, v_ref, x1s_ref, wo_ref,
                     y_ref, ys_ref, os_all, *, sb, steps):
    i = pl.program_id(0)
    for t in range(sb):
        row0 = pl.multiple_of((i * sb + t) * HEAD_ROWS, HEAD_ROWS)
        os_all[pl.ds(row0, HEAD_ROWS), :] = _attend_one(q_ref[t], k_ref[t], v_ref[t])
    y_ref[...] = _ffn(x2_ref[...], gffn_ref, wg_ref, wu_ref, wd_ref, gfin_ref)

    @pl.when(i == steps - 1)
    def _():
        n = x1s_ref.shape[0]
        cols = {_head_row_cols(r): os_all[pl.ds(r, n, stride=HEAD_ROWS), :] for r in range(HEAD_ROWS)}
        o = jnp.concatenate([cols[c] for c in sorted(cols)], axis=-1).astype(BF16)
        x2s = x1s_ref[...] + _dot(o, wo_ref[...])
        _store_row_major(ys_ref, _ffn(x2s, gffn_ref, wg_ref, wu_ref, wd_ref, gfin_ref))


def _to_head_rows(t):
    lead = t.shape[:-2]
    t = t.reshape(lead + (MEM_HEADS, 2, LANES))
    return jnp.swapaxes(t, -3, -2).reshape(lead + (HEAD_ROWS, LANES))


def _from_head_rows(t):
    lead = t.shape[:-2]
    t = t.reshape(lead + (2, MEM_HEADS, LANES))
    return jnp.swapaxes(t, -3, -2).reshape(lead + (MEM_HEADS, MEM_HEAD_DIM))


def _ffn_attn(x2, w, tm, q8, k, v, x1s):
    rows = x2.shape[0]
    steps = rows // tm
    n = x1s.shape[0]
    sb = n // steps
    assert sb * steps == n
    row = pl.BlockSpec((tm, D_MODEL), lambda i: (i, 0))
    consts = [w["g_ffn"], w["w_gate"], w["w_up"], w["w_down"], w["g_final"]]
    blk = pl.BlockSpec((sb, N_MEM, HEAD_ROWS, LANES), lambda i: (i, 0, 0, 0))
    vec = pl.BlockSpec((sb, HEAD_ROWS, LANES), lambda i: (i, 0, 0))
    return pl.pallas_call(
        functools.partial(_ffn_attn_kernel, sb=sb, steps=steps),
        grid=(steps,),
        in_specs=[row] + [_const_spec(c.shape) for c in consts] + [vec, blk, blk]
                 + [_const_spec(x1s.shape), _const_spec(w["w_o"].shape)],
        out_specs=[row, _const_spec((n * SUBLANES, LANES))],
        out_shape=[jax.ShapeDtypeStruct((rows, D_MODEL), F32), jax.ShapeDtypeStruct((n * SUBLANES, LANES), F32)],
        scratch_shapes=[pltpu.VMEM((n * HEAD_ROWS, LANES), F32)],
        compiler_params=pltpu.CompilerParams(dimension_semantics=("arbitrary",), vmem_limit_bytes=VMEM_LIMIT),
        name="ffn_attn",
    )(x2, *consts, q8.reshape(n, HEAD_ROWS, LANES), _to_head_rows(k), _to_head_rows(v), x1s, w["w_o"])


def _prep_weights(g_mix, w_in, pool_map_w, pool_map_b, pool_scale, conv_dw_w, conv_dw_b, conv_ln_g,
                  conv_ln_b, w_out, g_attn, g_mem, w_q, w_k, w_v, w_o, g_ffn, w_gate, w_up, w_down,
                  g_final, l):
    vec = lambda v: v.reshape(1, -1)
    pm = pool_map_w[l]
    z = jnp.zeros((POOL_GROUP_W, POOL_GROUP_W), F32)
    wmap = jnp.stack([jnp.block([[pm[0], z], [z, pm[1]]]), jnp.block([[pm[2], z], [z, pm[3]]])])
    wi = w_in[l]
    slabs = [wi[:, POOL_WIDTH + o + c * LANES:POOL_WIDTH + o + (c + 1) * LANES]
             for c in range(CONV_SLABS) for o in (0, CONV_CH)]
    w_in_grouped = jnp.concatenate(slabs + [wi[:, :POOL_WIDTH]], axis=1)
    return dict(
        g_mix=vec(g_mix[l]), w_in=w_in_grouped.astype(BF16), wmap=wmap.astype(BF16), b_map=vec(pool_map_b[l]),
        p_scale=vec(pool_scale[l]),
        w_dw=jnp.broadcast_to(conv_dw_w[l][:, None, :], (CONV_K, SUBLANES, CONV_CH)),
        b_dw=vec(conv_dw_b[l]), ln_g=vec(conv_ln_g[l]), ln_b=vec(conv_ln_b[l]),
        w_out=w_out[l].astype(BF16), g_attn=vec(g_attn[l]), g_mem=vec(g_mem[l]),
        w_q=w_q[l].astype(BF16), w_k=w_k[l].astype(BF16), w_v=w_v[l].astype(BF16),
        w_o=w_o[l].astype(BF16), g_ffn=vec(g_ffn[l]), g_final=vec(g_final))


def kernel(x_prompt, x_sample, mem_prompt, state_pool, state_conv, cache_mem_k, cache_mem_v, g_mix, w_in,
           pool_map_w, pool_map_b, pool_scale, conv_dw_w, conv_dw_b, conv_ln_g, conv_ln_b, w_out, g_attn,
           g_mem, w_q, w_k, w_v, w_o, g_ffn, w_gate, w_up, w_down, g_final):
    assert state_pool.shape[0] == 1, "single-layer trunk"
    nb, seq, _ = x_prompt.shape
    ns = x_sample.shape[0]
    w = _prep_weights(g_mix, w_in, pool_map_w, pool_map_b, pool_scale, conv_dw_w, conv_dw_b, conv_ln_g,
                      conv_ln_b, w_out, g_attn, g_mem, w_q, w_k, w_v, w_o, g_ffn, w_gate, w_up, w_down,
                      g_final, 0)

    x1s, qs, pool_s, conv_s = _sample_pre(x_sample.reshape(ns * SUBLANES, LANES), jnp.swapaxes(state_pool[0], 0, 1),
                                          jnp.swapaxes(state_conv[0], 0, 1), w)
    pool_s = jnp.swapaxes(pool_s, 0, 1)
    conv_s = jnp.swapaxes(conv_s, 0, 1)

    mk, mv, kt, vb = _mem_kv(mem_prompt, w["g_mem"], w["w_k"], w["w_v"])
    ffn_f32 = dict(w_gate=w_gate[0], w_up=w_up[0], w_down=w_down[0])
    x2p, pool_p, conv_p, w["w_gate"], w["w_up"], w["w_down"] = _mixer(x_prompt, kt, vb, w, ffn_f32, tm=1024)
    yp, ys = _ffn_attn(x2p, w, 512, qs, cache_mem_k[0], cache_mem_v[0], x1s)

    return (yp.reshape(nb, seq, D_MODEL), ys.reshape(ns, 1, D_MODEL),
            pool_p[None], pool_s[None], conv_p[None], conv_s[None],
            _from_head_rows(mk.reshape(nb, N_MEM, HEAD_ROWS, LANES))[None],
            _from_head_rows(mv.reshape(nb, N_MEM, HEAD_ROWS, LANES))[None])
```

```python
import functools
import math

import jax
import jax.numpy as jnp
from jax import lax
from jax.experimental import pallas as pl
from jax.experimental.pallas import tpu as pltpu

D_MODEL = 1024
POOL_WINDOWS = (2, 4, 8, 16)
POOL_GROUP_W = 128
POOL_WIDTH = 512
POOL_HIST = 15
CONV_CH = 512
CONV_K = 31
CONV_HIST = 30
IN_COLS = POOL_WIDTH + 2 * CONV_CH
N_MEM = 256
MEM_HEADS = 4
MEM_HEAD_DIM = 256
EPS = 1e-6
ATTN_SCALE = 1.0 / math.sqrt(MEM_HEAD_DIM)
Q_SCALE = ATTN_SCALE * math.log2(math.e)

SUBLANES = 8
LANES = 128
POOL_SLABS = POOL_WIDTH // LANES
CONV_SLABS = CONV_CH // LANES
HEAD_ROWS = 2 * MEM_HEADS
CONV_ROWS = 32
POOL_PAD = 16
CONV_PAD = 32
VMEM_LIMIT = 60 * 1024 * 1024

BF16 = jnp.bfloat16
F32 = jnp.float32


def _rms(x, g):
    ms = jnp.mean(x * x, axis=-1, keepdims=True)
    return x * lax.rsqrt(ms + EPS) * g


def _dot(a, b):
    return jnp.dot(a, b, preferred_element_type=F32)


def _silu(x):
    return x * jax.nn.sigmoid(x)


def _layernorm_silu(y, g, b):
    mu = jnp.mean(y, axis=-1, keepdims=True)
    yc = y - mu
    var = jnp.mean(yc * yc, axis=-1, keepdims=True)
    return _silu(yc * lax.rsqrt(var + EPS) * g + b)


def _pool_map(d, wmap_ref, bmap, pscale):
    db = d.astype(BF16)
    y = jnp.concatenate([_dot(db[:, :256], wmap_ref[0]), _dot(db[:, 256:], wmap_ref[1])], axis=-1)
    return (y + bmap) * pscale


def _load_row_major(ref, n):
    return jnp.concatenate([ref[pl.ds(j, n, stride=SUBLANES), :] for j in range(D_MODEL // LANES)], axis=-1)


def _store_row_major(ref, val):
    for j in range(D_MODEL // LANES):
        ref[pl.ds(j, val.shape[0], stride=SUBLANES), :] = val[:, j * LANES:(j + 1) * LANES]


def _head_row_cols(r):
    half, head = divmod(r, MEM_HEADS)
    return head * MEM_HEAD_DIM + half * LANES


def _const_spec(shape):
    nd = len(shape)
    return pl.BlockSpec(shape, lambda *_: (0,) * nd, pipeline_mode=pl.Buffered(1))


def _mem_kv_kernel(mem_ref, g_ref, wk32_ref, wv32_ref, win32_ref, wout32_ref, wq32_ref, wo32_ref,
                   k_ref, v_ref, kt_ref, vb_ref, win16_ref, wout16_ref, wq16_ref, wo16_ref, wk16, wv16):
    @pl.when(pl.program_id(0) == 0)
    def _():
        wk16[...] = wk32_ref[...].astype(BF16)
        wv16[...] = wv32_ref[...].astype(BF16)

    for c in range(CONV_SLABS):
        for o, src0 in enumerate((POOL_WIDTH, POOL_WIDTH + CONV_CH)):
            win16_ref[:, (2 * c + o) * LANES:(2 * c + o + 1) * LANES] = (
                win32_ref[:, src0 + c * LANES:src0 + (c + 1) * LANES].astype(BF16))
    win16_ref[:, 2 * CONV_CH:] = win32_ref[:, :POOL_WIDTH].astype(BF16)
    wout16_ref[...] = wout32_ref[...].astype(BF16)
    wq16_ref[...] = wq32_ref[...].astype(BF16)
    wo16_ref[...] = wo32_ref[...].astype(BF16)

    nbb = mem_ref.shape[0]
    m = _rms(mem_ref[...].reshape(nbb * N_MEM, D_MODEL), g_ref[...]).astype(BF16)
    k = _dot(m, wk16[...])
    v = _dot(m, wv16[...])
    for b in range(nbb):
        kb = k[b * N_MEM:(b + 1) * N_MEM]
        vb = v[b * N_MEM:(b + 1) * N_MEM]
        for r in range(HEAD_ROWS):
            c0 = _head_row_cols(r)
            k_ref[b, pl.ds(r, N_MEM, stride=HEAD_ROWS), :] = kb[:, c0:c0 + LANES]
            v_ref[b, pl.ds(r, N_MEM, stride=HEAD_ROWS), :] = vb[:, c0:c0 + LANES]
        kt_ref[b] = kb.T.astype(BF16)
        vb_ref[b] = vb.astype(BF16)


def _mem_kv(mem, g_mem, wk, wv, side, nbb=2):
    nb = mem.shape[0]
    steps = nb // nbb
    blk = lambda shape: pl.BlockSpec(shape, lambda b: (b, 0, 0))
    rows_blk = blk((nbb, N_MEM * HEAD_ROWS, LANES))
    side_specs = [pl.BlockSpec((t.shape[0] // steps, t.shape[1]), lambda b: (b, 0)) for t in side]
    return pl.pallas_call(
        _mem_kv_kernel,
        grid=(steps,),
        in_specs=[blk((nbb, N_MEM, D_MODEL)), _const_spec((1, D_MODEL)),
                  _const_spec((D_MODEL, D_MODEL)), _const_spec((D_MODEL, D_MODEL))] + side_specs,
        out_specs=[rows_blk, rows_blk, blk((nbb, D_MODEL, N_MEM)), blk((nbb, N_MEM, D_MODEL))] + side_specs,
        out_shape=[jax.ShapeDtypeStruct((nb, N_MEM * HEAD_ROWS, LANES), F32),
                   jax.ShapeDtypeStruct((nb, N_MEM * HEAD_ROWS, LANES), F32),
                   jax.ShapeDtypeStruct((nb, D_MODEL, N_MEM), BF16),
                   jax.ShapeDtypeStruct((nb, N_MEM, D_MODEL), BF16)]
                  + [jax.ShapeDtypeStruct(t.shape, BF16) for t in side],
        scratch_shapes=[pltpu.VMEM((D_MODEL, D_MODEL), BF16), pltpu.VMEM((D_MODEL, D_MODEL), BF16)],
        compiler_params=pltpu.CompilerParams(dimension_semantics=("arbitrary",),
                                             vmem_limit_bytes=VMEM_LIMIT),
        name="mem_kv",
    )(mem, g_mem, wk, wv, *side)


def _mixer_kernel(x_ref, xres_ref, kt_ref, vb_ref, gmix_ref, win_ref, wmap_ref, bmap_ref, pscale_ref,
                  wdw_ref, bdw_ref, lng_ref, lnb_ref, wout_ref, gattn_ref, wq_ref, wo_ref,
                  wg32_ref, wu32_ref, wd32_ref,
                  x2_ref, pst_ref, cst_ref, wg16_ref, wu16_ref, wd16_ref,
                  a_ext, g_ext, conv_buf, mix_buf, *, tm, nj, n_tiles):
    s = pl.program_id(0)
    j = s % nj

    @pl.when(s == 0)
    def _():
        mix_buf[...] = jnp.zeros(mix_buf.shape, BF16)

    @pl.when(j == 0)
    def _():
        a_ext[:, 0:POOL_PAD, :] = jnp.zeros((POOL_SLABS, POOL_PAD, LANES), F32)
        g_ext[:, 0:CONV_PAD, :] = jnp.zeros((CONV_SLABS, CONV_PAD, LANES), F32)

    def project_and_attend():
        x1 = xres_ref[...] + _dot(mix_buf[...], wout_ref[...])
        q = (_dot(_rms(x1, gattn_ref[...]).astype(BF16), wq_ref[...]) * Q_SCALE).astype(BF16)
        heads = []
        for hd in range(MEM_HEADS):
            c0 = hd * MEM_HEAD_DIM
            sc = _dot(q[:, c0:c0 + MEM_HEAD_DIM], kt_ref[0, c0:c0 + MEM_HEAD_DIM, :])
            e = jnp.exp2(sc - jnp.max(sc, axis=-1, keepdims=True))
            p = (e * (1.0 / jnp.sum(e, axis=-1, keepdims=True))).astype(BF16)
            heads.append(_dot(p, vb_ref[0, :, c0:c0 + MEM_HEAD_DIM]).astype(BF16))
        x2_ref[...] = x1 + _dot(jnp.concatenate(heads, axis=-1), wo_ref[...])

    @pl.when(s < n_tiles)
    def _():
        wg16_ref[...] = wg32_ref[...].astype(BF16)
        wu16_ref[...] = wu32_ref[...].astype(BF16)
        wd16_ref[...] = wd32_ref[...].astype(BF16)

        h = _rms(x_ref[...], gmix_ref[...]).astype(BF16)
        for c in range(CONV_SLABS):
            u = _dot(h, win_ref[:, 2 * c * LANES:2 * (c + 1) * LANES])
            g_ext[c, CONV_PAD:CONV_PAD + tm, :] = u[:, :LANES] * jax.nn.sigmoid(u[:, LANES:])
        u = _dot(h, win_ref[:, 2 * CONV_CH:])
        for c in range(POOL_SLABS):
            a_ext[c, POOL_PAD:POOL_PAD + tm, :] = u[:, c * LANES:(c + 1) * LANES]

        project_and_attend()

        pos = j * tm + lax.broadcasted_iota(jnp.int32, (tm, 1), 0)
        ds = []
        for g, w in enumerate(POOL_WINDOWS):
            a_g = a_ext[g, POOL_PAD:POOL_PAD + tm, :]
            win = a_g
            for i in range(1, w):
                win = win + a_ext[g, POOL_PAD - i:POOL_PAD - i + tm, :]
            inv_cnt = 1.0 / jnp.minimum(w, pos + 1).astype(F32)
            ds.append(win * inv_cnt - a_g)
        ya = _pool_map(jnp.concatenate(ds, axis=-1), wmap_ref, bmap_ref[...], pscale_ref[...])

        for c in range(CONV_SLABS):
            for t0 in range(0, tm, CONV_ROWS):
                acc = None
                for k in range(CONV_K):
                    r0 = CONV_PAD - CONV_HIST + t0 + k
                    g = g_ext[c, r0:r0 + CONV_ROWS, :].reshape(CONV_ROWS // SUBLANES, SUBLANES, LANES)
                    term = g * wdw_ref[k, :, c * LANES:(c + 1) * LANES][None]
                    acc = term if acc is None else acc + term
                conv_buf[t0:t0 + CONV_ROWS, c * LANES:(c + 1) * LANES] = acc.reshape(CONV_ROWS, LANES)
        yb = _layernorm_silu(conv_buf[...] + bdw_ref[...], lng_ref[...], lnb_ref[...])
        mix_buf[...] = jnp.concatenate([ya, yb], axis=-1).astype(BF16)

        for c in range(POOL_SLABS):
            pst_ref[0, :, c * LANES:(c + 1) * LANES] = a_ext[c, POOL_PAD + tm - POOL_HIST:POOL_PAD + tm, :]
            a_ext[c, 0:POOL_PAD, :] = a_ext[c, tm:tm + POOL_PAD, :]
        for c in range(CONV_SLABS):
            cst_ref[0, :, c * LANES:(c + 1) * LANES] = g_ext[c, CONV_PAD + tm - CONV_HIST:CONV_PAD + tm, :]
            g_ext[c, 0:CONV_PAD, :] = g_ext[c, tm:tm + CONV_PAD, :]

    @pl.when(s == n_tiles)
    def _():
        project_and_attend()


def _mixer(x, kt, vb, w, ffn_w, tm):
    nb, seq, _ = x.shape
    nj = seq // tm
    n_tiles = nb * nj
    x2d = x.reshape(nb * seq, D_MODEL)
    cur = lambda s: jnp.minimum(s, n_tiles - 1)
    prev = lambda s: jnp.maximum(s - 1, 0)
    row_cur = pl.BlockSpec((tm, D_MODEL), lambda s: (cur(s), 0))
    row_prev = pl.BlockSpec((tm, D_MODEL), lambda s: (prev(s), 0))
    per_b = lambda shape, tile: pl.BlockSpec(shape, lambda s: (tile(s) // nj, 0, 0))
    consts = [w["g_mix"], w["w_in"], w["wmap"], w["b_map"], w["p_scale"], w["w_dw"], w["b_dw"],
              w["ln_g"], w["ln_b"], w["w_out"], w["g_attn"], w["w_q"], w["w_o"]]
    ffn32 = [ffn_w["w_gate"], ffn_w["w_up"], ffn_w["w_down"]]
    ffn_specs = [pl.BlockSpec((t.shape[0] // n_tiles, t.shape[1]), lambda s: (cur(s), 0)) for t in ffn32]
    return pl.pallas_call(
        functools.partial(_mixer_kernel, tm=tm, nj=nj, n_tiles=n_tiles),
        grid=(n_tiles + 1,),
        in_specs=[row_cur, row_prev, per_b((1, D_MODEL, N_MEM), prev), per_b((1, N_MEM, D_MODEL), prev)]
                 + [_const_spec(c.shape) for c in consts] + ffn_specs,
        out_specs=[row_prev, per_b((1, POOL_HIST, POOL_WIDTH), cur),
                   per_b((1, CONV_HIST, CONV_CH), cur)] + ffn_specs,
        out_shape=[jax.ShapeDtypeStruct((nb * seq, D_MODEL), F32),
                   jax.ShapeDtypeStruct((nb, POOL_HIST, POOL_WIDTH), F32),
                   jax.ShapeDtypeStruct((nb, CONV_HIST, CONV_CH), F32)]
                  + [jax.ShapeDtypeStruct(t.shape, BF16) for t in ffn32],
        scratch_shapes=[pltpu.VMEM((POOL_SLABS, POOL_PAD + tm, LANES), F32),
                        pltpu.VMEM((CONV_SLABS, CONV_PAD + tm, LANES), F32),
                        pltpu.VMEM((tm, CONV_CH), F32),
                        pltpu.VMEM((tm, D_MODEL), BF16)],
        compiler_params=pltpu.CompilerParams(dimension_semantics=("arbitrary",),
                                             vmem_limit_bytes=VMEM_LIMIT),
        name="mixer",
    )(x2d, x2d, kt, vb, *consts, *ffn32)


def _sample_pre_kernel(x_ref, sp_ref, sc_ref, gmix_ref, win_ref, wmap_ref, bmap_ref, pscale_ref,
                       wdw_ref, bdw_ref, lng_ref, lnb_ref, wout_ref, gattn_ref, wq_ref,
                       x1_ref, q_ref, pst_ref, cst_ref):
    n = x1_ref.shape[0]
    x = _load_row_major(x_ref, n)
    h = _rms(x, gmix_ref[...]).astype(BF16)
    u = _dot(h, win_ref[...])
    a = u[:, 2 * CONV_CH:]
    glu = jnp.concatenate([u[:, 2 * c * LANES:(2 * c + 1) * LANES] * jax.nn.sigmoid(u[:, (2 * c + 1) * LANES:2 * (c + 1) * LANES])
                           for c in range(CONV_SLABS)], axis=-1)

    ds = []
    for g, w in enumerate(POOL_WINDOWS):
        c0 = g * POOL_GROUP_W
        a_g = a[:, c0:c0 + POOL_GROUP_W]
        win = a_g
        for i in range(1, w):
            win = win + sp_ref[POOL_HIST - i, :, c0:c0 + POOL_GROUP_W]
        ds.append(win * (1.0 / w) - a_g)
    ya = _pool_map(jnp.concatenate(ds, axis=-1), wmap_ref, bmap_ref[...], pscale_ref[...])

    conv = glu * wdw_ref[CONV_HIST, 0:1, :] + bdw_ref[...]
    for k in range(CONV_HIST):
        conv = conv + sc_ref[k] * wdw_ref[k, 0:1, :]
    yb = _layernorm_silu(conv, lng_ref[...], lnb_ref[...])

    pst_ref[0:POOL_HIST - 1] = sp_ref[1:POOL_HIST]
    pst_ref[POOL_HIST - 1] = a
    cst_ref[0:CONV_HIST - 1] = sc_ref[1:CONV_HIST]
    cst_ref[CONV_HIST - 1] = glu

    mix = jnp.concatenate([ya, yb], axis=-1).astype(BF16)
    x1 = x + _dot(mix, wout_ref[...])
    x1_ref[...] = x1
    q = _dot(_rms(x1, gattn_ref[...]).astype(BF16), wq_ref[...]) * Q_SCALE
    for r in range(HEAD_ROWS):
        c0 = _head_row_cols(r)
        q_ref[pl.ds(r, n, stride=HEAD_ROWS), :] = q[:, c0:c0 + LANES]


def _sample_pre(x, sp, sc, w):
    n = sp.shape[1]
    consts = [w["g_mix"], w["w_in"], w["wmap"], w["b_map"], w["p_scale"], w["w_dw"], w["b_dw"],
              w["ln_g"], w["ln_b"], w["w_out"], w["g_attn"], w["w_q"]]
    args = [x, sp, sc] + consts
    return pl.pallas_call(
        _sample_pre_kernel,
        grid=(1,),
        in_specs=[_const_spec(t.shape) for t in args],
        out_specs=[_const_spec((n, D_MODEL)), _const_spec((n * HEAD_ROWS, LANES)),
                   _const_spec((POOL_HIST, n, POOL_WIDTH)), _const_spec((CONV_HIST, n, CONV_CH))],
        out_shape=[jax.ShapeDtypeStruct((n, D_MODEL), F32), jax.ShapeDtypeStruct((n * HEAD_ROWS, LANES), F32),
                   jax.ShapeDtypeStruct((POOL_HIST, n, POOL_WIDTH), F32),
                   jax.ShapeDtypeStruct((CONV_HIST, n, CONV_CH), F32)],
        compiler_params=pltpu.CompilerParams(dimension_semantics=("arbitrary",),
                                             vmem_limit_bytes=VMEM_LIMIT),
        name="sample_pre",
    )(*args)


def _attend_one(q, k, v):
    kq = k * q[None]
    s = jnp.sum(kq + pltpu.roll(kq, MEM_HEADS, axis=1), axis=-1, keepdims=True)
    e = jnp.exp2(s - jnp.max(s, axis=0, keepdims=True))
    return jnp.sum(e * v, axis=0) / jnp.sum(e, axis=0)


def _ffn(x2, gffn_ref, wg_ref, wu_ref, wd_ref, gfin_ref):
    h = _rms(x2, gffn_ref[...]).astype(BF16)
    act = (_silu(_dot(h, wg_ref[...])) * _dot(h, wu_ref[...])).astype(BF16)
    x3 = x2 + _dot(act, wd_ref[...])
    return _rms(x3, gfin_ref[...])


def _ffn_attn_kernel(x2_ref, gffn_ref, wg_ref, wu_ref, wd_ref, gfin_ref, q_ref, k_ref, v_ref, x1s_ref, wo_ref,
                     y_ref, ys_ref, os_all, *, sb, steps):
    i = pl.program_id(0)
    for t in range(sb):
        row0 = pl.multiple_of((i * sb + t) * HEAD_ROWS, HEAD_ROWS)
        os_all[pl.ds(row0, HEAD_ROWS), :] = _attend_one(q_ref[t], k_ref[t], v_ref[t])
    y_ref[...] = _ffn(x2_ref[...], gffn_ref, wg_ref, wu_ref, wd_ref, gfin_ref)

    @pl.when(i == steps - 1)
    def _():
        n = x1s_ref.shape[0]
        cols = {_head_row_cols(r): os_all[pl.ds(r, n, stride=HEAD_ROWS), :] for r in range(HEAD_ROWS)}
        o = jnp.concatenate([cols[c] for c in sorted(cols)], axis=-1).astype(BF16)
        x2s = x1s_ref[...] + _dot(o, wo_ref[...])
        _store_row_major(ys_ref, _ffn(x2s, gffn_ref, wg_ref, wu_ref, wd_ref, gfin_ref))


def _to_head_rows(t):
    lead = t.shape[:-2]
    t = t.reshape(lead + (MEM_HEADS, 2, LANES))
    return jnp.swapaxes(t, -3, -2).reshape(lead + (HEAD_ROWS, LANES))


def _from_head_rows(t):
    lead = t.shape[:-2]
    t = t.reshape(lead + (2, MEM_HEADS, LANES))
    return jnp.swapaxes(t, -3, -2).reshape(lead + (MEM_HEADS, MEM_HEAD_DIM))


def _ffn_attn(x2, w, tm, q8, k, v, x1s):
    rows = x2.shape[0]
    steps = rows // tm
    n = x1s.shape[0]
    sb = n // steps
    assert sb * steps == n
    row = pl.BlockSpec((tm, D_MODEL), lambda i: (i, 0))
    consts = [w["g_ffn"], w["w_gate"], w["w_up"], w["w_down"], w["g_final"]]
    blk = pl.BlockSpec((sb, N_MEM, HEAD_ROWS, LANES), lambda i: (i, 0, 0, 0))
    vec = pl.BlockSpec((sb, HEAD_ROWS, LANES), lambda i: (i, 0, 0))
    return pl.pallas_call(
        functools.partial(_ffn_attn_kernel, sb=sb, steps=steps),
        grid=(steps,),
        in_specs=[row] + [_const_spec(c.shape) for c in consts] + [vec, blk, blk]
                 + [_const_spec(x1s.shape), _const_spec(w["w_o"].shape)],
        out_specs=[row, _const_spec((n * SUBLANES, LANES))],
        out_shape=[jax.ShapeDtypeStruct((rows, D_MODEL), F32), jax.ShapeDtypeStruct((n * SUBLANES, LANES), F32)],
        scratch_shapes=[pltpu.VMEM((n * HEAD_ROWS, LANES), F32)],
        compiler_params=pltpu.CompilerParams(dimension_semantics=("arbitrary",), vmem_limit_bytes=VMEM_LIMIT),
        name="ffn_attn",
    )(x2, *consts, q8.reshape(n, HEAD_ROWS, LANES), _to_head_rows(k), _to_head_rows(v), x1s, w["w_o"])


def _prep_weights(g_mix, w_in, pool_map_w, pool_map_b, pool_scale, conv_dw_w, conv_dw_b, conv_ln_g,
                  conv_ln_b, w_out, g_attn, g_mem, w_q, w_k, w_v, w_o, g_ffn, w_gate, w_up, w_down,
                  g_final, l):
    vec = lambda v: v.reshape(1, -1)
    pm = pool_map_w[l]
    z = jnp.zeros((POOL_GROUP_W, POOL_GROUP_W), F32)
    wmap = jnp.stack([jnp.block([[pm[0], z], [z, pm[1]]]), jnp.block([[pm[2], z], [z, pm[3]]])])
    return dict(
        g_mix=vec(g_mix[l]), wmap=wmap.astype(BF16), b_map=vec(pool_map_b[l]),
        p_scale=vec(pool_scale[l]),
        w_dw=jnp.broadcast_to(conv_dw_w[l][:, None, :], (CONV_K, SUBLANES, CONV_CH)),
        b_dw=vec(conv_dw_b[l]), ln_g=vec(conv_ln_g[l]), ln_b=vec(conv_ln_b[l]),
        g_attn=vec(g_attn[l]), g_mem=vec(g_mem[l]), g_ffn=vec(g_ffn[l]), g_final=vec(g_final))


def kernel(x_prompt, x_sample, mem_prompt, state_pool, state_conv, cache_mem_k, cache_mem_v, g_mix, w_in,
           pool_map_w, pool_map_b, pool_scale, conv_dw_w, conv_dw_b, conv_ln_g, conv_ln_b, w_out, g_attn,
           g_mem, w_q, w_k, w_v, w_o, g_ffn, w_gate, w_up, w_down, g_final):
    assert state_pool.shape[0] == 1, "single-layer trunk"
    nb, seq, _ = x_prompt.shape
    ns = x_sample.shape[0]
    w = _prep_weights(g_mix, w_in, pool_map_w, pool_map_b, pool_scale, conv_dw_w, conv_dw_b, conv_ln_g,
                      conv_ln_b, w_out, g_attn, g_mem, w_q, w_k, w_v, w_o, g_ffn, w_gate, w_up, w_down,
                      g_final, 0)

    mk, mv, kt, vb, w["w_in"], w["w_out"], w["w_q"], w["w_o"] = _mem_kv(
        mem_prompt, w["g_mem"], w_k[0], w_v[0], [w_in[0], w_out[0], w_q[0], w_o[0]])

    x1s, qs, pool_s, conv_s = _sample_pre(x_sample.reshape(ns * SUBLANES, LANES), jnp.swapaxes(state_pool[0], 0, 1),
                                          jnp.swapaxes(state_conv[0], 0, 1), w)
    pool_s = jnp.swapaxes(pool_s, 0, 1)
    conv_s = jnp.swapaxes(conv_s, 0, 1)

    ffn_f32 = dict(w_gate=w_gate[0], w_up=w_up[0], w_down=w_down[0])
    x2p, pool_p, conv_p, w["w_gate"], w["w_up"], w["w_down"] = _mixer(x_prompt, kt, vb, w, ffn_f32, tm=1024)
    yp, ys = _ffn_attn(x2p, w, 512, qs, cache_mem_k[0], cache_mem_v[0], x1s)

    return (yp.reshape(nb, seq, D_MODEL), ys.reshape(ns, 1, D_MODEL),
            pool_p[None], pool_s[None], conv_p[None], conv_s[None],
            _from_head_rows(mk.reshape(nb, N_MEM, HEAD_ROWS, LANES))[None],
            _from_head_rows(mv.reshape(nb, N_MEM, HEAD_ROWS, LANES))[None])
```

```python
import functools
import math

import jax
import jax.numpy as jnp
from jax import lax
from jax.experimental import pallas as pl
from jax.experimental.pallas import tpu as pltpu

D_MODEL = 1024
POOL_WINDOWS = (2, 4, 8, 16)
POOL_GROUP_W = 128
POOL_WIDTH = 512
POOL_HIST = 15
CONV_CH = 512
CONV_K = 31
CONV_HIST = 30
IN_COLS = POOL_WIDTH + 2 * CONV_CH
N_MEM = 256
MEM_HEADS = 4
MEM_HEAD_DIM = 256
EPS = 1e-6
ATTN_SCALE = 1.0 / math.sqrt(MEM_HEAD_DIM)
Q_SCALE = ATTN_SCALE * math.log2(math.e)

SUBLANES = 8
LANES = 128
POOL_SLABS = POOL_WIDTH // LANES
CONV_SLABS = CONV_CH // LANES
HEAD_ROWS = 2 * MEM_HEADS
MIXER_ROWS = 1024
FFN_ROWS = 512
MEM_BATCH_ROWS = 2
SAMPLE_CHUNKS = 2
CONV_ROWS = 32
POOL_PAD = 16
CONV_PAD = 32
VMEM_LIMIT = 60 * 1024 * 1024

BF16 = jnp.bfloat16
F32 = jnp.float32


def _rms(x, g):
    ms = jnp.mean(x * x, axis=-1, keepdims=True)
    return x * lax.rsqrt(ms + EPS) * g


def _dot(a, b):
    return jnp.dot(a, b, preferred_element_type=F32)


def _silu(x):
    return x * jax.nn.sigmoid(x)


def _layernorm_silu(y, g, b):
    mu = jnp.mean(y, axis=-1, keepdims=True)
    yc = y - mu
    var = jnp.mean(yc * yc, axis=-1, keepdims=True)
    return _silu(yc * lax.rsqrt(var + EPS) * g + b)


def _pool_map(d, wmap_ref, bmap, pscale):
    db = d.astype(BF16)
    y = jnp.concatenate([_dot(db[:, :256], wmap_ref[0]), _dot(db[:, 256:], wmap_ref[1])], axis=-1)
    return (y + bmap) * pscale


def _load_row_major(ref, n):
    return jnp.concatenate([ref[pl.ds(j, n, stride=SUBLANES), :] for j in range(D_MODEL // LANES)], axis=-1)


def _store_row_major(ref, val):
    for j in range(D_MODEL // LANES):
        ref[pl.ds(j, val.shape[0], stride=SUBLANES), :] = val[:, j * LANES:(j + 1) * LANES]


def _head_row_cols(r):
    half, head = divmod(r, MEM_HEADS)
    return head * MEM_HEAD_DIM + half * LANES


def _const_spec(shape):
    nd = len(shape)
    return pl.BlockSpec(shape, lambda *_: (0,) * nd, pipeline_mode=pl.Buffered(1))


def _mem_kv_kernel(mem_ref, g_ref, wk32_ref, wv32_ref, win32_ref, wout32_ref, wq32_ref, wo32_ref,
                   k_ref, v_ref, kt_ref, vb_ref, win16_ref, wout16_ref, wq16_ref, wo16_ref, wk16, wv16):
    @pl.when(pl.program_id(0) == 0)
    def _():
        wk16[...] = wk32_ref[...].astype(BF16)
        wv16[...] = wv32_ref[...].astype(BF16)

    for c in range(CONV_SLABS):
        for o, src0 in enumerate((POOL_WIDTH, POOL_WIDTH + CONV_CH)):
            win16_ref[:, (2 * c + o) * LANES:(2 * c + o + 1) * LANES] = (
                win32_ref[:, src0 + c * LANES:src0 + (c + 1) * LANES].astype(BF16))
    win16_ref[:, 2 * CONV_CH:] = win32_ref[:, :POOL_WIDTH].astype(BF16)
    wout16_ref[...] = wout32_ref[...].astype(BF16)
    wq16_ref[...] = wq32_ref[...].astype(BF16)
    wo16_ref[...] = wo32_ref[...].astype(BF16)

    nbb = mem_ref.shape[0]
    m = _rms(mem_ref[...].reshape(nbb * N_MEM, D_MODEL), g_ref[...]).astype(BF16)
    k = _dot(m, wk16[...])
    v = _dot(m, wv16[...])
    for b in range(nbb):
        kb = k[b * N_MEM:(b + 1) * N_MEM]
        vb = v[b * N_MEM:(b + 1) * N_MEM]
        for r in range(HEAD_ROWS):
            c0 = _head_row_cols(r)
            k_ref[b, pl.ds(r, N_MEM, stride=HEAD_ROWS), :] = kb[:, c0:c0 + LANES]
            v_ref[b, pl.ds(r, N_MEM, stride=HEAD_ROWS), :] = vb[:, c0:c0 + LANES]
        kt_ref[b] = kb.T.astype(BF16)
        vb_ref[b] = vb.astype(BF16)


def _mem_kv(mem, g_mem, wk, wv, side, nbb):
    nb = mem.shape[0]
    steps = nb // nbb
    blk = lambda shape: pl.BlockSpec(shape, lambda b: (b, 0, 0))
    rows_blk = blk((nbb, N_MEM * HEAD_ROWS, LANES))
    side_specs = [pl.BlockSpec((t.shape[0] // steps, t.shape[1]), lambda b: (b, 0)) for t in side]
    return pl.pallas_call(
        _mem_kv_kernel,
        grid=(steps,),
        in_specs=[blk((nbb, N_MEM, D_MODEL)), _const_spec((1, D_MODEL)),
                  _const_spec((D_MODEL, D_MODEL)), _const_spec((D_MODEL, D_MODEL))] + side_specs,
        out_specs=[rows_blk, rows_blk, blk((nbb, D_MODEL, N_MEM)), blk((nbb, N_MEM, D_MODEL))] + side_specs,
        out_shape=[jax.ShapeDtypeStruct((nb, N_MEM * HEAD_ROWS, LANES), F32),
                   jax.ShapeDtypeStruct((nb, N_MEM * HEAD_ROWS, LANES), F32),
                   jax.ShapeDtypeStruct((nb, D_MODEL, N_MEM), BF16),
                   jax.ShapeDtypeStruct((nb, N_MEM, D_MODEL), BF16)]
                  + [jax.ShapeDtypeStruct(t.shape, BF16) for t in side],
        scratch_shapes=[pltpu.VMEM((D_MODEL, D_MODEL), BF16), pltpu.VMEM((D_MODEL, D_MODEL), BF16)],
        compiler_params=pltpu.CompilerParams(dimension_semantics=("arbitrary",),
                                             vmem_limit_bytes=VMEM_LIMIT),
        name="mem_kv",
    )(mem, g_mem, wk, wv, *side)


def _mixer_kernel(x_ref, xres_ref, kt_ref, vb_ref, gmix_ref, win_ref, wmap_ref, bmap_ref, pscale_ref,
                  wdw_ref, bdw_ref, lng_ref, lnb_ref, wout_ref, gattn_ref, wq_ref, wo_ref,
                  wg32_ref, wu32_ref, wd32_ref,
                  x2_ref, pst_ref, cst_ref, wg16_ref, wu16_ref, wd16_ref,
                  a_ext, g_ext, conv_buf, mix_buf, *, tm, nj, n_tiles):
    s = pl.program_id(0)
    j = s % nj

    @pl.when(s == 0)
    def _():
        mix_buf[...] = jnp.zeros(mix_buf.shape, BF16)

    @pl.when(j == 0)
    def _():
        a_ext[:, 0:POOL_PAD, :] = jnp.zeros((POOL_SLABS, POOL_PAD, LANES), F32)
        g_ext[:, 0:CONV_PAD, :] = jnp.zeros((CONV_SLABS, CONV_PAD, LANES), F32)

    def project_and_attend():
        x1 = xres_ref[...] + _dot(mix_buf[...], wout_ref[...])
        q = (_dot(_rms(x1, gattn_ref[...]).astype(BF16), wq_ref[...]) * Q_SCALE).astype(BF16)
        heads = []
        for hd in range(MEM_HEADS):
            c0 = hd * MEM_HEAD_DIM
            sc = _dot(q[:, c0:c0 + MEM_HEAD_DIM], kt_ref[0, c0:c0 + MEM_HEAD_DIM, :])
            e = jnp.exp2(sc - jnp.max(sc, axis=-1, keepdims=True))
            p = (e * (1.0 / jnp.sum(e, axis=-1, keepdims=True))).astype(BF16)
            heads.append(_dot(p, vb_ref[0, :, c0:c0 + MEM_HEAD_DIM]).astype(BF16))
        x2_ref[...] = x1 + _dot(jnp.concatenate(heads, axis=-1), wo_ref[...])

    @pl.when(s < n_tiles)
    def _():
        wg16_ref[...] = wg32_ref[...].astype(BF16)
        wu16_ref[...] = wu32_ref[...].astype(BF16)
        wd16_ref[...] = wd32_ref[...].astype(BF16)

        h = _rms(x_ref[...], gmix_ref[...]).astype(BF16)
        for c in range(CONV_SLABS):
            u = _dot(h, win_ref[:, 2 * c * LANES:2 * (c + 1) * LANES])
            g_ext[c, CONV_PAD:CONV_PAD + tm, :] = u[:, :LANES] * jax.nn.sigmoid(u[:, LANES:])
        u = _dot(h, win_ref[:, 2 * CONV_CH:])
        for c in range(POOL_SLABS):
            a_ext[c, POOL_PAD:POOL_PAD + tm, :] = u[:, c * LANES:(c + 1) * LANES]

        project_and_attend()

        pos = j * tm + lax.broadcasted_iota(jnp.int32, (tm, 1), 0)
        ds = []
        for g, w in enumerate(POOL_WINDOWS):
            a_g = a_ext[g, POOL_PAD:POOL_PAD + tm, :]
            win = a_g
            for i in range(1, w):
                win = win + a_ext[g, POOL_PAD - i:POOL_PAD - i + tm, :]
            inv_cnt = 1.0 / jnp.minimum(w, pos + 1).astype(F32)
            ds.append(win * inv_cnt - a_g)
        ya = _pool_map(jnp.concatenate(ds, axis=-1), wmap_ref, bmap_ref[...], pscale_ref[...])

        for c in range(CONV_SLABS):
            for t0 in range(0, tm, CONV_ROWS):
                acc = None
                for k in range(CONV_K):
                    r0 = CONV_PAD - CONV_HIST + t0 + k
                    g = g_ext[c, r0:r0 + CONV_ROWS, :].reshape(CONV_ROWS // SUBLANES, SUBLANES, LANES)
                    term = g * wdw_ref[k, :, c * LANES:(c + 1) * LANES][None]
                    acc = term if acc is None else acc + term
                conv_buf[t0:t0 + CONV_ROWS, c * LANES:(c + 1) * LANES] = acc.reshape(CONV_ROWS, LANES)
        yb = _layernorm_silu(conv_buf[...] + bdw_ref[...], lng_ref[...], lnb_ref[...])
        mix_buf[...] = jnp.concatenate([ya, yb], axis=-1).astype(BF16)

        for c in range(POOL_SLABS):
            pst_ref[0, :, c * LANES:(c + 1) * LANES] = a_ext[c, POOL_PAD + tm - POOL_HIST:POOL_PAD + tm, :]
            a_ext[c, 0:POOL_PAD, :] = a_ext[c, tm:tm + POOL_PAD, :]
        for c in range(CONV_SLABS):
            cst_ref[0, :, c * LANES:(c + 1) * LANES] = g_ext[c, CONV_PAD + tm - CONV_HIST:CONV_PAD + tm, :]
            g_ext[c, 0:CONV_PAD, :] = g_ext[c, tm:tm + CONV_PAD, :]

    @pl.when(s == n_tiles)
    def _():
        project_and_attend()


def _mixer(x, kt, vb, w, ffn_w, tm):
    nb, seq, _ = x.shape
    nj = seq // tm
    n_tiles = nb * nj
    x2d = x.reshape(nb * seq, D_MODEL)
    cur = lambda s: jnp.minimum(s, n_tiles - 1)
    prev = lambda s: jnp.maximum(s - 1, 0)
    row_cur = pl.BlockSpec((tm, D_MODEL), lambda s: (cur(s), 0))
    row_prev = pl.BlockSpec((tm, D_MODEL), lambda s: (prev(s), 0))
    per_b = lambda shape, tile: pl.BlockSpec(shape, lambda s: (tile(s) // nj, 0, 0))
    consts = [w["g_mix"], w["w_in"], w["wmap"], w["b_map"], w["p_scale"], w["w_dw"], w["b_dw"],
              w["ln_g"], w["ln_b"], w["w_out"], w["g_attn"], w["w_q"], w["w_o"]]
    ffn32 = [ffn_w["w_gate"], ffn_w["w_up"], ffn_w["w_down"]]
    ffn_specs = [pl.BlockSpec((t.shape[0] // n_tiles, t.shape[1]), lambda s: (cur(s), 0)) for t in ffn32]
    return pl.pallas_call(
        functools.partial(_mixer_kernel, tm=tm, nj=nj, n_tiles=n_tiles),
        grid=(n_tiles + 1,),
        in_specs=[row_cur, row_prev, per_b((1, D_MODEL, N_MEM), prev), per_b((1, N_MEM, D_MODEL), prev)]
                 + [_const_spec(c.shape) for c in consts] + ffn_specs,
        out_specs=[row_prev, per_b((1, POOL_HIST, POOL_WIDTH), cur),
                   per_b((1, CONV_HIST, CONV_CH), cur)] + ffn_specs,
        out_shape=[jax.ShapeDtypeStruct((nb * seq, D_MODEL), F32),
                   jax.ShapeDtypeStruct((nb, POOL_HIST, POOL_WIDTH), F32),
                   jax.ShapeDtypeStruct((nb, CONV_HIST, CONV_CH), F32)]
                  + [jax.ShapeDtypeStruct(t.shape, BF16) for t in ffn32],
        scratch_shapes=[pltpu.VMEM((POOL_SLABS, POOL_PAD + tm, LANES), F32),
                        pltpu.VMEM((CONV_SLABS, CONV_PAD + tm, LANES), F32),
                        pltpu.VMEM((tm, CONV_CH), F32),
                        pltpu.VMEM((tm, D_MODEL), BF16)],
        compiler_params=pltpu.CompilerParams(dimension_semantics=("arbitrary",),
                                             vmem_limit_bytes=VMEM_LIMIT),
        name="mixer",
    )(x2d, x2d, kt, vb, *consts, *ffn32)


def _sample_pre_kernel(x_ref, sp_ref, sc_ref, gmix_ref, win_ref, wmap_ref, bmap_ref, pscale_ref,
                       wdw_ref, bdw_ref, lng_ref, lnb_ref, wout_ref, gattn_ref, wq_ref,
                       x1_ref, q_ref, pst_ref, cst_ref):
    n = x1_ref.shape[0]
    x = _load_row_major(x_ref, n)
    h = _rms(x, gmix_ref[...]).astype(BF16)
    u = _dot(h, win_ref[...])
    a = u[:, 2 * CONV_CH:]
    glu = jnp.concatenate([u[:, 2 * c * LANES:(2 * c + 1) * LANES] * jax.nn.sigmoid(u[:, (2 * c + 1) * LANES:2 * (c + 1) * LANES])
                           for c in range(CONV_SLABS)], axis=-1)

    ds = []
    for g, w in enumerate(POOL_WINDOWS):
        c0 = g * POOL_GROUP_W
        a_g = a[:, c0:c0 + POOL_GROUP_W]
        win = a_g
        for i in range(1, w):
            win = win + sp_ref[POOL_HIST - i, :, c0:c0 + POOL_GROUP_W]
        ds.append(win * (1.0 / w) - a_g)
    ya = _pool_map(jnp.concatenate(ds, axis=-1), wmap_ref, bmap_ref[...], pscale_ref[...])

    conv = glu * wdw_ref[CONV_HIST, 0:1, :] + bdw_ref[...]
    for k in range(CONV_HIST):
        conv = conv + sc_ref[k] * wdw_ref[k, 0:1, :]
    yb = _layernorm_silu(conv, lng_ref[...], lnb_ref[...])

    pst_ref[0:POOL_HIST - 1] = sp_ref[1:POOL_HIST]
    pst_ref[POOL_HIST - 1] = a
    cst_ref[0:CONV_HIST - 1] = sc_ref[1:CONV_HIST]
    cst_ref[CONV_HIST - 1] = glu

    mix = jnp.concatenate([ya, yb], axis=-1).astype(BF16)
    x1 = x + _dot(mix, wout_ref[...])
    x1_ref[...] = x1
    q = _dot(_rms(x1, gattn_ref[...]).astype(BF16), wq_ref[...]) * Q_SCALE
    for r in range(HEAD_ROWS):
        c0 = _head_row_cols(r)
        q_ref[pl.ds(r, n, stride=HEAD_ROWS), :] = q[:, c0:c0 + LANES]


def _sample_pre(x, sp, sc, w):
    n = sp.shape[1]
    nc = n // SAMPLE_CHUNKS
    consts = [w["g_mix"], w["w_in"], w["wmap"], w["b_map"], w["p_scale"], w["w_dw"], w["b_dw"],
              w["ln_g"], w["ln_b"], w["w_out"], w["g_attn"], w["w_q"]]
    rows = lambda r, width: pl.BlockSpec((r, width), lambda i: (i, 0))
    hist = lambda h, width: pl.BlockSpec((h, nc, width), lambda i: (0, i, 0))
    return pl.pallas_call(
        _sample_pre_kernel,
        grid=(SAMPLE_CHUNKS,),
        in_specs=[rows(nc * SUBLANES, LANES), hist(POOL_HIST, POOL_WIDTH), hist(CONV_HIST, CONV_CH)]
                 + [_const_spec(t.shape) for t in consts],
        out_specs=[rows(nc, D_MODEL), rows(nc * HEAD_ROWS, LANES),
                   hist(POOL_HIST, POOL_WIDTH), hist(CONV_HIST, CONV_CH)],
        out_shape=[jax.ShapeDtypeStruct((n, D_MODEL), F32), jax.ShapeDtypeStruct((n * HEAD_ROWS, LANES), F32),
                   jax.ShapeDtypeStruct((POOL_HIST, n, POOL_WIDTH), F32),
                   jax.ShapeDtypeStruct((CONV_HIST, n, CONV_CH), F32)],
        compiler_params=pltpu.CompilerParams(dimension_semantics=("arbitrary",),
                                             vmem_limit_bytes=VMEM_LIMIT),
        name="sample_pre",
    )(x, sp, sc, *consts)


def _attend_one(q, k, v):
    kq = k * q[None]
    s = jnp.sum(kq + pltpu.roll(kq, MEM_HEADS, axis=1), axis=-1, keepdims=True)
    e = jnp.exp2(s - jnp.max(s, axis=0, keepdims=True))
    return jnp.sum(e * v, axis=0) / jnp.sum(e, axis=0)


def _ffn(x2, gffn_ref, wg_ref, wu_ref, wd_ref, gfin_ref):
    h = _rms(x2, gffn_ref[...]).astype(BF16)
    act = (_silu(_dot(h, wg_ref[...])) * _dot(h, wu_ref[...])).astype(BF16)
    x3 = x2 + _dot(act, wd_ref[...])
    return _rms(x3, gfin_ref[...])


def _ffn_attn_kernel(x2_ref, gffn_ref, wg_ref, wu_ref, wd_ref, gfin_ref, q_ref, k_ref, v_ref, x1s_ref, wo_ref,
                     y_ref, ys_ref, os_all, *, sb, steps):
    i = pl.program_id(0)
    for t in range(sb):
        row0 = pl.multiple_of((i * sb + t) * HEAD_ROWS, HEAD_ROWS)
        os_all[pl.ds(row0, HEAD_ROWS), :] = _attend_one(q_ref[t], k_ref[t], v_ref[t])
    y_ref[...] = _ffn(x2_ref[...], gffn_ref, wg_ref, wu_ref, wd_ref, gfin_ref)

    @pl.when(i == steps - 1)
    def _():
        n = x1s_ref.shape[0]
        cols = {_head_row_cols(r): os_all[pl.ds(r, n, stride=HEAD_ROWS), :] for r in range(HEAD_ROWS)}
        o = jnp.concatenate([cols[c] for c in sorted(cols)], axis=-1).astype(BF16)
        x2s = x1s_ref[...] + _dot(o, wo_ref[...])
        _store_row_major(ys_ref, _ffn(x2s, gffn_ref, wg_ref, wu_ref, wd_ref, gfin_ref))


def _to_head_rows(t):
    lead = t.shape[:-2]
    t = t.reshape(lead + (MEM_HEADS, 2, LANES))
    return jnp.swapaxes(t, -3, -2).reshape(lead + (HEAD_ROWS, LANES))


def _from_head_rows(t):
    lead = t.shape[:-2]
    t = t.reshape(lead + (2, MEM_HEADS, LANES))
    return jnp.swapaxes(t, -3, -2).reshape(lead + (MEM_HEADS, MEM_HEAD_DIM))


def _ffn_attn(x2, w, tm, q8, k, v, x1s):
    rows = x2.shape[0]
    steps = rows // tm
    n = x1s.shape[0]
    sb = n // steps
    assert sb * steps == n
    row = pl.BlockSpec((tm, D_MODEL), lambda i: (i, 0))
    consts = [w["g_ffn"], w["w_gate"], w["w_up"], w["w_down"], w["g_final"]]
    blk = pl.BlockSpec((sb, N_MEM, HEAD_ROWS, LANES), lambda i: (i, 0, 0, 0))
    vec = pl.BlockSpec((sb, HEAD_ROWS, LANES), lambda i: (i, 0, 0))
    return pl.pallas_call(
        functools.partial(_ffn_attn_kernel, sb=sb, steps=steps),
        grid=(steps,),
        in_specs=[row] + [_const_spec(c.shape) for c in consts] + [vec, blk, blk]
                 + [_const_spec(x1s.shape), _const_spec(w["w_o"].shape)],
        out_specs=[row, _const_spec((n * SUBLANES, LANES))],
        out_shape=[jax.ShapeDtypeStruct((rows, D_MODEL), F32), jax.ShapeDtypeStruct((n * SUBLANES, LANES), F32)],
        scratch_shapes=[pltpu.VMEM((n * HEAD_ROWS, LANES), F32)],
        compiler_params=pltpu.CompilerParams(dimension_semantics=("arbitrary",), vmem_limit_bytes=VMEM_LIMIT),
        name="ffn_attn",
    )(x2, *consts, q8.reshape(n, HEAD_ROWS, LANES), _to_head_rows(k), _to_head_rows(v), x1s, w["w_o"])


def _prep_weights(g_mix, w_in, pool_map_w, pool_map_b, pool_scale, conv_dw_w, conv_dw_b, conv_ln_g,
                  conv_ln_b, w_out, g_attn, g_mem, w_q, w_k, w_v, w_o, g_ffn, w_gate, w_up, w_down,
                  g_final, l):
    vec = lambda v: v.reshape(1, -1)
    pm = pool_map_w[l]
    z = jnp.zeros((POOL_GROUP_W, POOL_GROUP_W), F32)
    wmap = jnp.stack([jnp.block([[pm[0], z], [z, pm[1]]]), jnp.block([[pm[2], z], [z, pm[3]]])])
    return dict(
        g_mix=vec(g_mix[l]), wmap=wmap.astype(BF16), b_map=vec(pool_map_b[l]),
        p_scale=vec(pool_scale[l]),
        w_dw=jnp.broadcast_to(conv_dw_w[l][:, None, :], (CONV_K, SUBLANES, CONV_CH)),
        b_dw=vec(conv_dw_b[l]), ln_g=vec(conv_ln_g[l]), ln_b=vec(conv_ln_b[l]),
        g_attn=vec(g_attn[l]), g_mem=vec(g_mem[l]), g_ffn=vec(g_ffn[l]), g_final=vec(g_final))


def kernel(x_prompt, x_sample, mem_prompt, state_pool, state_conv, cache_mem_k, cache_mem_v, g_mix, w_in,
           pool_map_w, pool_map_b, pool_scale, conv_dw_w, conv_dw_b, conv_ln_g, conv_ln_b, w_out, g_attn,
           g_mem, w_q, w_k, w_v, w_o, g_ffn, w_gate, w_up, w_down, g_final):
    assert state_pool.shape[0] == 1, "single-layer trunk"
    nb, seq, _ = x_prompt.shape
    ns = x_sample.shape[0]
    w = _prep_weights(g_mix, w_in, pool_map_w, pool_map_b, pool_scale, conv_dw_w, conv_dw_b, conv_ln_g,
                      conv_ln_b, w_out, g_attn, g_mem, w_q, w_k, w_v, w_o, g_ffn, w_gate, w_up, w_down,
                      g_final, 0)

    mk, mv, kt, vb, w["w_in"], w["w_out"], w["w_q"], w["w_o"] = _mem_kv(
        mem_prompt, w["g_mem"], w_k[0], w_v[0], [w_in[0], w_out[0], w_q[0], w_o[0]], MEM_BATCH_ROWS)

    x1s, qs, pool_s, conv_s = _sample_pre(x_sample.reshape(ns * SUBLANES, LANES), jnp.swapaxes(state_pool[0], 0, 1),
                                          jnp.swapaxes(state_conv[0], 0, 1), w)
    pool_s = jnp.swapaxes(pool_s, 0, 1)
    conv_s = jnp.swapaxes(conv_s, 0, 1)

    ffn_f32 = dict(w_gate=w_gate[0], w_up=w_up[0], w_down=w_down[0])
    x2p, pool_p, conv_p, w["w_gate"], w["w_up"], w["w_down"] = _mixer(x_prompt, kt, vb, w, ffn_f32, MIXER_ROWS)
    yp, ys = _ffn_attn(x2p, w, FFN_ROWS, qs, cache_mem_k[0], cache_mem_v[0], x1s)

    return (yp.reshape(nb, seq, D_MODEL), ys.reshape(ns, 1, D_MODEL),
            pool_p[None], pool_s[None], conv_p[None], conv_s[None],
            _from_head_rows(mk.reshape(nb, N_MEM, HEAD_ROWS, LANES))[None],
            _from_head_rows(mv.reshape(nb, N_MEM, HEAD_ROWS, LANES))[None])
```

```python
import functools
import math

import jax
import jax.numpy as jnp
from jax import lax
from jax.experimental import pallas as pl
from jax.experimental.pallas import tpu as pltpu

D_MODEL = 1024
POOL_WINDOWS = (2, 4, 8, 16)
POOL_GROUP_W = 128
POOL_WIDTH = 512
POOL_HIST = 15
CONV_CH = 512
CONV_K = 31
CONV_HIST = 30
IN_COLS = POOL_WIDTH + 2 * CONV_CH
N_MEM = 256
MEM_HEADS = 4
MEM_HEAD_DIM = 256
EPS = 1e-6
ATTN_SCALE = 1.0 / math.sqrt(MEM_HEAD_DIM)
Q_SCALE = ATTN_SCALE * math.log2(math.e)

SUBLANES = 8
LANES = 128
POOL_SLABS = POOL_WIDTH // LANES
CONV_SLABS = CONV_CH // LANES
HEAD_ROWS = 2 * MEM_HEADS
MIXER_ROWS = 1024
FFN_ROWS = 512
MEM_BATCH_ROWS = 2
SAMPLE_CHUNKS = 2
CONV_ROWS = 32
POOL_PAD = 16
CONV_PAD = 32
VMEM_LIMIT = 60 * 1024 * 1024

BF16 = jnp.bfloat16
F32 = jnp.float32


def _rms(x, g):
    ms = jnp.mean(x * x, axis=-1, keepdims=True)
    return x * lax.rsqrt(ms + EPS) * g


def _dot(a, b):
    return jnp.dot(a, b, preferred_element_type=F32)


def _silu(x):
    return x * jax.nn.sigmoid(x)


def _layernorm_silu(y, g, b):
    mu = jnp.mean(y, axis=-1, keepdims=True)
    yc = y - mu
    var = jnp.mean(yc * yc, axis=-1, keepdims=True)
    return _silu(yc * lax.rsqrt(var + EPS) * g + b)


def _pool_map(d, wmap_ref, bmap, pscale):
    db = d.astype(BF16)
    y = jnp.concatenate([_dot(db[:, :256], wmap_ref[0]), _dot(db[:, 256:], wmap_ref[1])], axis=-1)
    return (y + bmap) * pscale


def _load_row_major(ref, n):
    return jnp.concatenate([ref[pl.ds(j, n, stride=SUBLANES), :] for j in range(D_MODEL // LANES)], axis=-1)


def _store_row_major(ref, val):
    for j in range(D_MODEL // LANES):
        ref[pl.ds(j, val.shape[0], stride=SUBLANES), :] = val[:, j * LANES:(j + 1) * LANES]


def _head_row_cols(r):
    half, head = divmod(r, MEM_HEADS)
    return head * MEM_HEAD_DIM + half * LANES


def _const_spec(shape):
    nd = len(shape)
    return pl.BlockSpec(shape, lambda *_: (0,) * nd, pipeline_mode=pl.Buffered(1))


def _mem_kv_kernel(mem_ref, g_ref, wk32_ref, wv32_ref, win32_ref, wout32_ref, wq32_ref, wo32_ref,
                   k_ref, v_ref, kt_ref, vb_ref, win16_ref, wout16_ref, wq16_ref, wo16_ref, wk16, wv16):
    @pl.when(pl.program_id(0) == 0)
    def _():
        wk16[...] = wk32_ref[...].astype(BF16)
        wv16[...] = wv32_ref[...].astype(BF16)

    for c in range(CONV_SLABS):
        for o, src0 in enumerate((POOL_WIDTH, POOL_WIDTH + CONV_CH)):
            win16_ref[:, (2 * c + o) * LANES:(2 * c + o + 1) * LANES] = (
                win32_ref[:, src0 + c * LANES:src0 + (c + 1) * LANES].astype(BF16))
    win16_ref[:, 2 * CONV_CH:] = win32_ref[:, :POOL_WIDTH].astype(BF16)
    wout16_ref[...] = wout32_ref[...].astype(BF16)
    wq16_ref[...] = wq32_ref[...].astype(BF16)
    wo16_ref[...] = wo32_ref[...].astype(BF16)

    nbb = mem_ref.shape[0]
    m = _rms(mem_ref[...].reshape(nbb * N_MEM, D_MODEL), g_ref[...]).astype(BF16)
    k = _dot(m, wk16[...])
    v = _dot(m, wv16[...])
    for b in range(nbb):
        kb = k[b * N_MEM:(b + 1) * N_MEM]
        vb = v[b * N_MEM:(b + 1) * N_MEM]
        for r in range(HEAD_ROWS):
            c0 = _head_row_cols(r)
            k_ref[b, pl.ds(r, N_MEM, stride=HEAD_ROWS), :] = kb[:, c0:c0 + LANES]
            v_ref[b, pl.ds(r, N_MEM, stride=HEAD_ROWS), :] = vb[:, c0:c0 + LANES]
        kt_ref[b] = kb.T.astype(BF16)
        vb_ref[b] = vb.astype(BF16)


def _mem_kv(mem, g_mem, wk, wv, side, nbb):
    nb = mem.shape[0]
    steps = nb // nbb
    blk = lambda shape: pl.BlockSpec(shape, lambda b: (b, 0, 0))
    rows_blk = blk((nbb, N_MEM * HEAD_ROWS, LANES))
    side_specs = [pl.BlockSpec((t.shape[0] // steps, t.shape[1]), lambda b: (b, 0)) for t in side]
    return pl.pallas_call(
        _mem_kv_kernel,
        grid=(steps,),
        in_specs=[blk((nbb, N_MEM, D_MODEL)), _const_spec((1, D_MODEL)),
                  _const_spec((D_MODEL, D_MODEL)), _const_spec((D_MODEL, D_MODEL))] + side_specs,
        out_specs=[rows_blk, rows_blk, blk((nbb, D_MODEL, N_MEM)), blk((nbb, N_MEM, D_MODEL))] + side_specs,
        out_shape=[jax.ShapeDtypeStruct((nb, N_MEM * HEAD_ROWS, LANES), F32),
                   jax.ShapeDtypeStruct((nb, N_MEM * HEAD_ROWS, LANES), F32),
                   jax.ShapeDtypeStruct((nb, D_MODEL, N_MEM), BF16),
                   jax.ShapeDtypeStruct((nb, N_MEM, D_MODEL), BF16)]
                  + [jax.ShapeDtypeStruct(t.shape, BF16) for t in side],
        scratch_shapes=[pltpu.VMEM((D_MODEL, D_MODEL), BF16), pltpu.VMEM((D_MODEL, D_MODEL), BF16)],
        compiler_params=pltpu.CompilerParams(dimension_semantics=("arbitrary",),
                                             vmem_limit_bytes=VMEM_LIMIT),
        name="mem_kv",
    )(mem, g_mem, wk, wv, *side)


def _mixer_kernel(x_ref, kt_ref, vb_ref, gmix_ref, win_ref, wmap_ref, bmap_ref, pscale_ref,
                  wdw_ref, bdw_ref, lng_ref, lnb_ref, wout_ref, gattn_ref, wq_ref, wo_ref,
                  wg32_ref, wu32_ref, wd32_ref,
                  x2_ref, pst_ref, cst_ref, wg16_ref, wu16_ref, wd16_ref,
                  a_ext, g_ext, conv_buf, mix_buf, x_prev, *, tm, nj, n_tiles):
    s = pl.program_id(0)
    j = s % nj

    @pl.when(s == 0)
    def _():
        mix_buf[...] = jnp.zeros(mix_buf.shape, BF16)
        x_prev[...] = jnp.zeros(x_prev.shape, F32)

    @pl.when(j == 0)
    def _():
        a_ext[:, 0:POOL_PAD, :] = jnp.zeros((POOL_SLABS, POOL_PAD, LANES), F32)
        g_ext[:, 0:CONV_PAD, :] = jnp.zeros((CONV_SLABS, CONV_PAD, LANES), F32)

    def project_and_attend():
        x1 = x_prev[...] + _dot(mix_buf[...], wout_ref[...])
        q = (_dot(_rms(x1, gattn_ref[...]).astype(BF16), wq_ref[...]) * Q_SCALE).astype(BF16)
        heads = []
        for hd in range(MEM_HEADS):
            c0 = hd * MEM_HEAD_DIM
            sc = _dot(q[:, c0:c0 + MEM_HEAD_DIM], kt_ref[0, c0:c0 + MEM_HEAD_DIM, :])
            e = jnp.exp2(sc - jnp.max(sc, axis=-1, keepdims=True))
            p = (e * (1.0 / jnp.sum(e, axis=-1, keepdims=True))).astype(BF16)
            heads.append(_dot(p, vb_ref[0, :, c0:c0 + MEM_HEAD_DIM]).astype(BF16))
        x2_ref[...] = x1 + _dot(jnp.concatenate(heads, axis=-1), wo_ref[...])

    @pl.when(s < n_tiles)
    def _():
        wg16_ref[...] = wg32_ref[...].astype(BF16)
        wu16_ref[...] = wu32_ref[...].astype(BF16)
        wd16_ref[...] = wd32_ref[...].astype(BF16)

        h = _rms(x_ref[...], gmix_ref[...]).astype(BF16)
        for c in range(CONV_SLABS):
            u = _dot(h, win_ref[:, 2 * c * LANES:2 * (c + 1) * LANES])
            g_ext[c, CONV_PAD:CONV_PAD + tm, :] = u[:, :LANES] * jax.nn.sigmoid(u[:, LANES:])
        u = _dot(h, win_ref[:, 2 * CONV_CH:])
        for c in range(POOL_SLABS):
            a_ext[c, POOL_PAD:POOL_PAD + tm, :] = u[:, c * LANES:(c + 1) * LANES]

        project_and_attend()

        pos = j * tm + lax.broadcasted_iota(jnp.int32, (tm, 1), 0)
        ds = []
        for g, w in enumerate(POOL_WINDOWS):
            a_g = a_ext[g, POOL_PAD:POOL_PAD + tm, :]
            win = a_g
            for i in range(1, w):
                win = win + a_ext[g, POOL_PAD - i:POOL_PAD - i + tm, :]
            inv_cnt = 1.0 / jnp.minimum(w, pos + 1).astype(F32)
            ds.append(win * inv_cnt - a_g)
        ya = _pool_map(jnp.concatenate(ds, axis=-1), wmap_ref, bmap_ref[...], pscale_ref[...])

        for c in range(CONV_SLABS):
            for t0 in range(0, tm, CONV_ROWS):
                acc = None
                for k in range(CONV_K):
                    r0 = CONV_PAD - CONV_HIST + t0 + k
                    g = g_ext[c, r0:r0 + CONV_ROWS, :].reshape(CONV_ROWS // SUBLANES, SUBLANES, LANES)
                    term = g * wdw_ref[k, :, c * LANES:(c + 1) * LANES][None]
                    acc = term if acc is None else acc + term
                conv_buf[t0:t0 + CONV_ROWS, c * LANES:(c + 1) * LANES] = acc.reshape(CONV_ROWS, LANES)
        yb = _layernorm_silu(conv_buf[...] + bdw_ref[...], lng_ref[...], lnb_ref[...])
        mix_buf[...] = jnp.concatenate([ya, yb], axis=-1).astype(BF16)
        x_prev[...] = x_ref[...]

        for c in range(POOL_SLABS):
            pst_ref[0, :, c * LANES:(c + 1) * LANES] = a_ext[c, POOL_PAD + tm - POOL_HIST:POOL_PAD + tm, :]
            a_ext[c, 0:POOL_PAD, :] = a_ext[c, tm:tm + POOL_PAD, :]
        for c in range(CONV_SLABS):
            cst_ref[0, :, c * LANES:(c + 1) * LANES] = g_ext[c, CONV_PAD + tm - CONV_HIST:CONV_PAD + tm, :]
            g_ext[c, 0:CONV_PAD, :] = g_ext[c, tm:tm + CONV_PAD, :]

    @pl.when(s == n_tiles)
    def _():
        project_and_attend()


def _mixer(x, kt, vb, w, ffn_w, tm):
    nb, seq, _ = x.shape
    nj = seq // tm
    n_tiles = nb * nj
    x2d = x.reshape(nb * seq, D_MODEL)
    cur = lambda s: jnp.minimum(s, n_tiles - 1)
    prev = lambda s: jnp.maximum(s - 1, 0)
    row_cur = pl.BlockSpec((tm, D_MODEL), lambda s: (cur(s), 0))
    row_prev = pl.BlockSpec((tm, D_MODEL), lambda s: (prev(s), 0))
    per_b = lambda shape, tile: pl.BlockSpec(shape, lambda s: (tile(s) // nj, 0, 0))
    consts = [w["g_mix"], w["w_in"], w["wmap"], w["b_map"], w["p_scale"], w["w_dw"], w["b_dw"],
              w["ln_g"], w["ln_b"], w["w_out"], w["g_attn"], w["w_q"], w["w_o"]]
    ffn32 = [ffn_w["w_gate"], ffn_w["w_up"], ffn_w["w_down"]]
    ffn_specs = [pl.BlockSpec((t.shape[0] // n_tiles, t.shape[1]), lambda s: (cur(s), 0)) for t in ffn32]
    return pl.pallas_call(
        functools.partial(_mixer_kernel, tm=tm, nj=nj, n_tiles=n_tiles),
        grid=(n_tiles + 1,),
        in_specs=[row_cur, per_b((1, D_MODEL, N_MEM), prev), per_b((1, N_MEM, D_MODEL), prev)]
                 + [_const_spec(c.shape) for c in consts] + ffn_specs,
        out_specs=[row_prev, per_b((1, POOL_HIST, POOL_WIDTH), cur),
                   per_b((1, CONV_HIST, CONV_CH), cur)] + ffn_specs,
        out_shape=[jax.ShapeDtypeStruct((nb * seq, D_MODEL), F32),
                   jax.ShapeDtypeStruct((nb, POOL_HIST, POOL_WIDTH), F32),
                   jax.ShapeDtypeStruct((nb, CONV_HIST, CONV_CH), F32)]
                  + [jax.ShapeDtypeStruct(t.shape, BF16) for t in ffn32],
        scratch_shapes=[pltpu.VMEM((POOL_SLABS, POOL_PAD + tm, LANES), F32),
                        pltpu.VMEM((CONV_SLABS, CONV_PAD + tm, LANES), F32),
                        pltpu.VMEM((tm, CONV_CH), F32),
                        pltpu.VMEM((tm, D_MODEL), BF16),
                        pltpu.VMEM((tm, D_MODEL), F32)],
        compiler_params=pltpu.CompilerParams(dimension_semantics=("arbitrary",),
                                             vmem_limit_bytes=VMEM_LIMIT),
        name="mixer",
    )(x2d, kt, vb, *consts, *ffn32)


def _sample_pre_kernel(x_ref, sp_ref, sc_ref, gmix_ref, win_ref, wmap_ref, bmap_ref, pscale_ref,
                       wdw_ref, bdw_ref, lng_ref, lnb_ref, wout_ref, gattn_ref, wq_ref,
                       x1_ref, q_ref, pst_ref, cst_ref):
    n = x1_ref.shape[0]
    x = _load_row_major(x_ref, n)
    h = _rms(x, gmix_ref[...]).astype(BF16)
    u = _dot(h, win_ref[...])
    a = u[:, 2 * CONV_CH:]
    glu = jnp.concatenate([u[:, 2 * c * LANES:(2 * c + 1) * LANES] * jax.nn.sigmoid(u[:, (2 * c + 1) * LANES:2 * (c + 1) * LANES])
                           for c in range(CONV_SLABS)], axis=-1)

    ds = []
    for g, w in enumerate(POOL_WINDOWS):
        c0 = g * POOL_GROUP_W
        a_g = a[:, c0:c0 + POOL_GROUP_W]
        win = a_g
        for i in range(1, w):
            win = win + sp_ref[POOL_HIST - i, :, c0:c0 + POOL_GROUP_W]
        ds.append(win * (1.0 / w) - a_g)
    ya = _pool_map(jnp.concatenate(ds, axis=-1), wmap_ref, bmap_ref[...], pscale_ref[...])

    conv = glu * wdw_ref[CONV_HIST, 0:1, :] + bdw_ref[...]
    for k in range(CONV_HIST):
        conv = conv + sc_ref[k] * wdw_ref[k, 0:1, :]
    yb = _layernorm_silu(conv, lng_ref[...], lnb_ref[...])

    pst_ref[0:POOL_HIST - 1] = sp_ref[1:POOL_HIST]
    pst_ref[POOL_HIST - 1] = a
    cst_ref[0:CONV_HIST - 1] = sc_ref[1:CONV_HIST]
    cst_ref[CONV_HIST - 1] = glu

    mix = jnp.concatenate([ya, yb], axis=-1).astype(BF16)
    x1 = x + _dot(mix, wout_ref[...])
    x1_ref[...] = x1
    q = _dot(_rms(x1, gattn_ref[...]).astype(BF16), wq_ref[...]) * Q_SCALE
    for r in range(HEAD_ROWS):
        c0 = _head_row_cols(r)
        q_ref[pl.ds(r, n, stride=HEAD_ROWS), :] = q[:, c0:c0 + LANES]


def _sample_pre(x, sp, sc, w):
    n = sp.shape[1]
    nc = n // SAMPLE_CHUNKS
    consts = [w["g_mix"], w["w_in"], w["wmap"], w["b_map"], w["p_scale"], w["w_dw"], w["b_dw"],
              w["ln_g"], w["ln_b"], w["w_out"], w["g_attn"], w["w_q"]]
    rows = lambda r, width: pl.BlockSpec((r, width), lambda i: (i, 0))
    hist = lambda h, width: pl.BlockSpec((h, nc, width), lambda i: (0, i, 0))
    return pl.pallas_call(
        _sample_pre_kernel,
        grid=(SAMPLE_CHUNKS,),
        in_specs=[rows(nc * SUBLANES, LANES), hist(POOL_HIST, POOL_WIDTH), hist(CONV_HIST, CONV_CH)]
                 + [_const_spec(t.shape) for t in consts],
        out_specs=[rows(nc, D_MODEL), rows(nc * HEAD_ROWS, LANES),
                   hist(POOL_HIST, POOL_WIDTH), hist(CONV_HIST, CONV_CH)],
        out_shape=[jax.ShapeDtypeStruct((n, D_MODEL), F32), jax.ShapeDtypeStruct((n * HEAD_ROWS, LANES), F32),
                   jax.ShapeDtypeStruct((POOL_HIST, n, POOL_WIDTH), F32),
                   jax.ShapeDtypeStruct((CONV_HIST, n, CONV_CH), F32)],
        compiler_params=pltpu.CompilerParams(dimension_semantics=("arbitrary",),
                                             vmem_limit_bytes=VMEM_LIMIT),
        name="sample_pre",
    )(x, sp, sc, *consts)


def _attend_one(q, k, v):
    kq = k * q[None]
    s = jnp.sum(kq + pltpu.roll(kq, MEM_HEADS, axis=1), axis=-1, keepdims=True)
    e = jnp.exp2(s - jnp.max(s, axis=0, keepdims=True))
    return jnp.sum(e * v, axis=0) / jnp.sum(e, axis=0)


def _ffn(x2, gffn_ref, wg_ref, wu_ref, wd_ref, gfin_ref):
    h = _rms(x2, gffn_ref[...]).astype(BF16)
    act = (_silu(_dot(h, wg_ref[...])) * _dot(h, wu_ref[...])).astype(BF16)
    x3 = x2 + _dot(act, wd_ref[...])
    return _rms(x3, gfin_ref[...])


def _ffn_attn_kernel(x2_ref, gffn_ref, wg_ref, wu_ref, wd_ref, gfin_ref, q_ref, k_ref, v_ref, x1s_ref, wo_ref,
                     y_ref, ys_ref, os_all, *, sb, steps):
    i = pl.program_id(0)
    for t in range(sb):
        row0 = pl.multiple_of((i * sb + t) * HEAD_ROWS, HEAD_ROWS)
        os_all[pl.ds(row0, HEAD_ROWS), :] = _attend_one(q_ref[t], k_ref[t], v_ref[t])
    y_ref[...] = _ffn(x2_ref[...], gffn_ref, wg_ref, wu_ref, wd_ref, gfin_ref)

    @pl.when(i == steps - 1)
    def _():
        n = x1s_ref.shape[0]
        cols = {_head_row_cols(r): os_all[pl.ds(r, n, stride=HEAD_ROWS), :] for r in range(HEAD_ROWS)}
        o = jnp.concatenate([cols[c] for c in sorted(cols)], axis=-1).astype(BF16)
        x2s = x1s_ref[...] + _dot(o, wo_ref[...])
        _store_row_major(ys_ref, _ffn(x2s, gffn_ref, wg_ref, wu_ref, wd_ref, gfin_ref))


def _to_head_rows(t):
    lead = t.shape[:-2]
    t = t.reshape(lead + (MEM_HEADS, 2, LANES))
    return jnp.swapaxes(t, -3, -2).reshape(lead + (HEAD_ROWS, LANES))


def _from_head_rows(t):
    lead = t.shape[:-2]
    t = t.reshape(lead + (2, MEM_HEADS, LANES))
    return jnp.swapaxes(t, -3, -2).reshape(lead + (MEM_HEADS, MEM_HEAD_DIM))


def _ffn_attn(x2, w, tm, q8, k, v, x1s):
    rows = x2.shape[0]
    steps = rows // tm
    n = x1s.shape[0]
    sb = n // steps
    assert sb * steps == n
    row = pl.BlockSpec((tm, D_MODEL), lambda i: (i, 0))
    consts = [w["g_ffn"], w["w_gate"], w["w_up"], w["w_down"], w["g_final"]]
    blk = pl.BlockSpec((sb, N_MEM, HEAD_ROWS, LANES), lambda i: (i, 0, 0, 0))
    vec = pl.BlockSpec((sb, HEAD_ROWS, LANES), lambda i: (i, 0, 0))
    return pl.pallas_call(
        functools.partial(_ffn_attn_kernel, sb=sb, steps=steps),
        grid=(steps,),
        in_specs=[row] + [_const_spec(c.shape) for c in consts] + [vec, blk, blk]
                 + [_const_spec(x1s.shape), _const_spec(w["w_o"].shape)],
        out_specs=[row, _const_spec((n * SUBLANES, LANES))],
        out_shape=[jax.ShapeDtypeStruct((rows, D_MODEL), F32), jax.ShapeDtypeStruct((n * SUBLANES, LANES), F32)],
        scratch_shapes=[pltpu.VMEM((n * HEAD_ROWS, LANES), F32)],
        compiler_params=pltpu.CompilerParams(dimension_semantics=("arbitrary",), vmem_limit_bytes=VMEM_LIMIT),
        name="ffn_attn",
    )(x2, *consts, q8.reshape(n, HEAD_ROWS, LANES), _to_head_rows(k), _to_head_rows(v), x1s, w["w_o"])


def _prep_weights(g_mix, w_in, pool_map_w, pool_map_b, pool_scale, conv_dw_w, conv_dw_b, conv_ln_g,
                  conv_ln_b, w_out, g_attn, g_mem, w_q, w_k, w_v, w_o, g_ffn, w_gate, w_up, w_down,
                  g_final, l):
    vec = lambda v: v.reshape(1, -1)
    pm = pool_map_w[l]
    z = jnp.zeros((POOL_GROUP_W, POOL_GROUP_W), F32)
    wmap = jnp.stack([jnp.block([[pm[0], z], [z, pm[1]]]), jnp.block([[pm[2], z], [z, pm[3]]])])
    return dict(
        g_mix=vec(g_mix[l]), wmap=wmap.astype(BF16), b_map=vec(pool_map_b[l]),
        p_scale=vec(pool_scale[l]),
        w_dw=jnp.broadcast_to(conv_dw_w[l][:, None, :], (CONV_K, SUBLANES, CONV_CH)),
        b_dw=vec(conv_dw_b[l]), ln_g=vec(conv_ln_g[l]), ln_b=vec(conv_ln_b[l]),
        g_attn=vec(g_attn[l]), g_mem=vec(g_mem[l]), g_ffn=vec(g_ffn[l]), g_final=vec(g_final))


def kernel(x_prompt, x_sample, mem_prompt, state_pool, state_conv, cache_mem_k, cache_mem_v, g_mix, w_in,
           pool_map_w, pool_map_b, pool_scale, conv_dw_w, conv_dw_b, conv_ln_g, conv_ln_b, w_out, g_attn,
           g_mem, w_q, w_k, w_v, w_o, g_ffn, w_gate, w_up, w_down, g_final):
    assert state_pool.shape[0] == 1, "single-layer trunk"
    nb, seq, _ = x_prompt.shape
    ns = x_sample.shape[0]
    w = _prep_weights(g_mix, w_in, pool_map_w, pool_map_b, pool_scale, conv_dw_w, conv_dw_b, conv_ln_g,
                      conv_ln_b, w_out, g_attn, g_mem, w_q, w_k, w_v, w_o, g_ffn, w_gate, w_up, w_down,
                      g_final, 0)

    mk, mv, kt, vb, w["w_in"], w["w_out"], w["w_q"], w["w_o"] = _mem_kv(
        mem_prompt, w["g_mem"], w_k[0], w_v[0], [w_in[0], w_out[0], w_q[0], w_o[0]], MEM_BATCH_ROWS)

    x1s, qs, pool_s, conv_s = _sample_pre(x_sample.reshape(ns * SUBLANES, LANES), jnp.swapaxes(state_pool[0], 0, 1),
                                          jnp.swapaxes(state_conv[0], 0, 1), w)
    pool_s = jnp.swapaxes(pool_s, 0, 1)
    conv_s = jnp.swapaxes(conv_s, 0, 1)

    ffn_f32 = dict(w_gate=w_gate[0], w_up=w_up[0], w_down=w_down[0])
    x2p, pool_p, conv_p, w["w_gate"], w["w_up"], w["w_down"] = _mixer(x_prompt, kt, vb, w, ffn_f32, MIXER_ROWS)
    yp, ys = _ffn_attn(x2p, w, FFN_ROWS, qs, cache_mem_k[0], cache_mem_v[0], x1s)

    return (yp.reshape(nb, seq, D_MODEL), ys.reshape(ns, 1, D_MODEL),
            pool_p[None], pool_s[None], conv_p[None], conv_s[None],
            _from_head_rows(mk.reshape(nb, N_MEM, HEAD_ROWS, LANES))[None],
            _from_head_rows(mv.reshape(nb, N_MEM, HEAD_ROWS, LANES))[None])
```

```python
import functools
import math

import jax
import jax.numpy as jnp
from jax import lax
from jax.experimental import pallas as pl
from jax.experimental.pallas import tpu as pltpu

D_MODEL = 1024
POOL_WINDOWS = (2, 4, 8, 16)
POOL_GROUP_W = 128
POOL_WIDTH = 512
POOL_HIST = 15
CONV_CH = 512
CONV_K = 31
CONV_HIST = 30
IN_COLS = POOL_WIDTH + 2 * CONV_CH
N_MEM = 256
MEM_HEADS = 4
MEM_HEAD_DIM = 256
EPS = 1e-6
ATTN_SCALE = 1.0 / math.sqrt(MEM_HEAD_DIM)
Q_SCALE = ATTN_SCALE * math.log2(math.e)

SUBLANES = 8
LANES = 128
POOL_SLABS = POOL_WIDTH // LANES
CONV_SLABS = CONV_CH // LANES
HEAD_ROWS = 2 * MEM_HEADS
CONV_ROWS = 32
POOL_PAD = 16
CONV_PAD = 32
VMEM_LIMIT = 60 * 1024 * 1024

BF16 = jnp.bfloat16
F32 = jnp.float32


def _rms(x, g):
    ms = jnp.mean(x * x, axis=-1, keepdims=True)
    return x * lax.rsqrt(ms + EPS) * g


def _dot(a, b):
    return jnp.dot(a, b, preferred_element_type=F32)


def _silu(x):
    return x * jax.nn.sigmoid(x)


def _layernorm_silu(y, g, b):
    mu = jnp.mean(y, axis=-1, keepdims=True)
    yc = y - mu
    var = jnp.mean(yc * yc, axis=-1, keepdims=True)
    return _silu(yc * lax.rsqrt(var + EPS) * g + b)


def _pool_map(d, wmap_ref, bmap, pscale):
    db = d.astype(BF16)
    y = jnp.concatenate([_dot(db[:, :256], wmap_ref[0]), _dot(db[:, 256:], wmap_ref[1])], axis=-1)
    return (y + bmap) * pscale


def _load_row_major(ref, n):
    return jnp.concatenate([ref[pl.ds(j, n, stride=SUBLANES), :] for j in range(D_MODEL // LANES)], axis=-1)


def _store_row_major(ref, val):
    for j in range(D_MODEL // LANES):
        ref[pl.ds(j, val.shape[0], stride=SUBLANES), :] = val[:, j * LANES:(j + 1) * LANES]


def _head_row_cols(r):
    half, head = divmod(r, MEM_HEADS)
    return head * MEM_HEAD_DIM + half * LANES


def _const_spec(shape):
    nd = len(shape)
    return pl.BlockSpec(shape, lambda *_: (0,) * nd, pipeline_mode=pl.Buffered(1))


def _mem_kv_kernel(mem_ref, g_ref, wk32_ref, wv32_ref, pm_ref, wdw_ref, win32_ref, wout32_ref, wq32_ref, wo32_ref,
                   k_ref, v_ref, kt_ref, vb_ref, wmap_ref, wdw8_ref, win16_ref, wout16_ref, wq16_ref, wo16_ref,
                   wk16, wv16):
    @pl.when(pl.program_id(0) == 0)
    def _():
        wk16[...] = wk32_ref[...].astype(BF16)
        wv16[...] = wv32_ref[...].astype(BF16)
        wmap_ref[...] = jnp.zeros(wmap_ref.shape, BF16)
        for g in range(len(POOL_WINDOWS)):
            d0 = (g % 2) * POOL_GROUP_W
            wmap_ref[g // 2, d0:d0 + POOL_GROUP_W, d0:d0 + POOL_GROUP_W] = pm_ref[g].astype(BF16)
        for k in range(CONV_K):
            wdw8_ref[k] = jnp.broadcast_to(wdw_ref[k:k + 1, :], (SUBLANES, CONV_CH))

    for c in range(CONV_SLABS):
        for o, src0 in enumerate((POOL_WIDTH, POOL_WIDTH + CONV_CH)):
            win16_ref[:, (2 * c + o) * LANES:(2 * c + o + 1) * LANES] = (
                win32_ref[:, src0 + c * LANES:src0 + (c + 1) * LANES].astype(BF16))
    win16_ref[:, 2 * CONV_CH:] = win32_ref[:, :POOL_WIDTH].astype(BF16)
    wout16_ref[...] = wout32_ref[...].astype(BF16)
    wq16_ref[...] = wq32_ref[...].astype(BF16)
    wo16_ref[...] = wo32_ref[...].astype(BF16)

    nbb = mem_ref.shape[0]
    m = _rms(mem_ref[...].reshape(nbb * N_MEM, D_MODEL), g_ref[...]).astype(BF16)
    k = _dot(m, wk16[...])
    v = _dot(m, wv16[...])
    for b in range(nbb):
        kb = k[b * N_MEM:(b + 1) * N_MEM]
        vb = v[b * N_MEM:(b + 1) * N_MEM]
        for r in range(HEAD_ROWS):
            c0 = _head_row_cols(r)
            k_ref[b, pl.ds(r, N_MEM, stride=HEAD_ROWS), :] = kb[:, c0:c0 + LANES]
            v_ref[b, pl.ds(r, N_MEM, stride=HEAD_ROWS), :] = vb[:, c0:c0 + LANES]
        kt_ref[b] = kb.T.astype(BF16)
        vb_ref[b] = vb.astype(BF16)


def _mem_kv(mem, g_mem, wk, wv, pool_map, conv_taps, side, nbb=2):
    nb = mem.shape[0]
    steps = nb // nbb
    blk = lambda shape: pl.BlockSpec(shape, lambda b: (b, 0, 0))
    rows_blk = blk((nbb, N_MEM * HEAD_ROWS, LANES))
    side_specs = [pl.BlockSpec((t.shape[0] // steps, t.shape[1]), lambda b: (b, 0)) for t in side]
    wmap_shape = (len(POOL_WINDOWS) // 2, 2 * POOL_GROUP_W, 2 * POOL_GROUP_W)
    wdw8_shape = (CONV_K, SUBLANES, CONV_CH)
    return pl.pallas_call(
        _mem_kv_kernel,
        grid=(steps,),
        in_specs=[blk((nbb, N_MEM, D_MODEL)), _const_spec((1, D_MODEL)),
                  _const_spec((D_MODEL, D_MODEL)), _const_spec((D_MODEL, D_MODEL)),
                  _const_spec(pool_map.shape), _const_spec(conv_taps.shape)] + side_specs,
        out_specs=[rows_blk, rows_blk, blk((nbb, D_MODEL, N_MEM)), blk((nbb, N_MEM, D_MODEL)),
                   _const_spec(wmap_shape), _const_spec(wdw8_shape)] + side_specs,
        out_shape=[jax.ShapeDtypeStruct((nb, N_MEM * HEAD_ROWS, LANES), F32),
                   jax.ShapeDtypeStruct((nb, N_MEM * HEAD_ROWS, LANES), F32),
                   jax.ShapeDtypeStruct((nb, D_MODEL, N_MEM), BF16),
                   jax.ShapeDtypeStruct((nb, N_MEM, D_MODEL), BF16),
                   jax.ShapeDtypeStruct(wmap_shape, BF16), jax.ShapeDtypeStruct(wdw8_shape, F32)]
                  + [jax.ShapeDtypeStruct(t.shape, BF16) for t in side],
        scratch_shapes=[pltpu.VMEM((D_MODEL, D_MODEL), BF16), pltpu.VMEM((D_MODEL, D_MODEL), BF16)],
        compiler_params=pltpu.CompilerParams(dimension_semantics=("arbitrary",),
                                             vmem_limit_bytes=VMEM_LIMIT),
        name="mem_kv",
    )(mem, g_mem, wk, wv, pool_map, conv_taps, *side)


def _mixer_kernel(x_ref, xres_ref, kt_ref, vb_ref, gmix_ref, win_ref, wmap_ref, bmap_ref, pscale_ref,
                  wdw_ref, bdw_ref, lng_ref, lnb_ref, wout_ref, gattn_ref, wq_ref, wo_ref,
                  wg32_ref, wu32_ref, wd32_ref,
                  x2_ref, pst_ref, cst_ref, wg16_ref, wu16_ref, wd16_ref,
                  a_ext, g_ext, conv_buf, mix_buf, *, tm, nj, n_tiles):
    s = pl.program_id(0)
    j = s % nj

    @pl.when(s == 0)
    def _():
        mix_buf[...] = jnp.zeros(mix_buf.shape, BF16)

    @pl.when(j == 0)
    def _():
        a_ext[:, 0:POOL_PAD, :] = jnp.zeros((POOL_SLABS, POOL_PAD, LANES), F32)
        g_ext[:, 0:CONV_PAD, :] = jnp.zeros((CONV_SLABS, CONV_PAD, LANES), F32)

    def project_and_attend():
        x1 = xres_ref[...] + _dot(mix_buf[...], wout_ref[...])
        q = (_dot(_rms(x1, gattn_ref[...]).astype(BF16), wq_ref[...]) * Q_SCALE).astype(BF16)
        heads = []
        for hd in range(MEM_HEADS):
            c0 = hd * MEM_HEAD_DIM
            sc = _dot(q[:, c0:c0 + MEM_HEAD_DIM], kt_ref[0, c0:c0 + MEM_HEAD_DIM, :])
            e = jnp.exp2(sc - jnp.max(sc, axis=-1, keepdims=True))
            p = (e * (1.0 / jnp.sum(e, axis=-1, keepdims=True))).astype(BF16)
            heads.append(_dot(p, vb_ref[0, :, c0:c0 + MEM_HEAD_DIM]).astype(BF16))
        x2_ref[...] = x1 + _dot(jnp.concatenate(heads, axis=-1), wo_ref[...])

    @pl.when(s < n_tiles)
    def _():
        wg16_ref[...] = wg32_ref[...].astype(BF16)
        wu16_ref[...] = wu32_ref[...].astype(BF16)
        wd16_ref[...] = wd32_ref[...].astype(BF16)

        h = _rms(x_ref[...], gmix_ref[...]).astype(BF16)
        for c in range(CONV_SLABS):
            u = _dot(h, win_ref[:, 2 * c * LANES:2 * (c + 1) * LANES])
            g_ext[c, CONV_PAD:CONV_PAD + tm, :] = u[:, :LANES] * jax.nn.sigmoid(u[:, LANES:])
        u = _dot(h, win_ref[:, 2 * CONV_CH:])
        for c in range(POOL_SLABS):
            a_ext[c, POOL_PAD:POOL_PAD + tm, :] = u[:, c * LANES:(c + 1) * LANES]

        project_and_attend()

        pos = j * tm + lax.broadcasted_iota(jnp.int32, (tm, 1), 0)
        ds = []
        for g, w in enumerate(POOL_WINDOWS):
            a_g = a_ext[g, POOL_PAD:POOL_PAD + tm, :]
            win = a_g
            for i in range(1, w):
                win = win + a_ext[g, POOL_PAD - i:POOL_PAD - i + tm, :]
            inv_cnt = 1.0 / jnp.minimum(w, pos + 1).astype(F32)
            ds.append(win * inv_cnt - a_g)
        ya = _pool_map(jnp.concatenate(ds, axis=-1), wmap_ref, bmap_ref[...], pscale_ref[...])

        for c in range(CONV_SLABS):
            for t0 in range(0, tm, CONV_ROWS):
                acc = None
                for k in range(CONV_K):
                    r0 = CONV_PAD - CONV_HIST + t0 + k
                    g = g_ext[c, r0:r0 + CONV_ROWS, :].reshape(CONV_ROWS // SUBLANES, SUBLANES, LANES)
                    term = g * wdw_ref[k, :, c * LANES:(c + 1) * LANES][None]
                    acc = term if acc is None else acc + term
                conv_buf[t0:t0 + CONV_ROWS, c * LANES:(c + 1) * LANES] = acc.reshape(CONV_ROWS, LANES)
        yb = _layernorm_silu(conv_buf[...] + bdw_ref[...], lng_ref[...], lnb_ref[...])
        mix_buf[...] = jnp.concatenate([ya, yb], axis=-1).astype(BF16)

        for c in range(POOL_SLABS):
            pst_ref[0, :, c * LANES:(c + 1) * LANES] = a_ext[c, POOL_PAD + tm - POOL_HIST:POOL_PAD + tm, :]
            a_ext[c, 0:POOL_PAD, :] = a_ext[c, tm:tm + POOL_PAD, :]
        for c in range(CONV_SLABS):
            cst_ref[0, :, c * LANES:(c + 1) * LANES] = g_ext[c, CONV_PAD + tm - CONV_HIST:CONV_PAD + tm, :]
            g_ext[c, 0:CONV_PAD, :] = g_ext[c, tm:tm + CONV_PAD, :]

    @pl.when(s == n_tiles)
    def _():
        project_and_attend()


def _mixer(x, kt, vb, w, ffn_w, tm):
    nb, seq, _ = x.shape
    nj = seq // tm
    n_tiles = nb * nj
    x2d = x.reshape(nb * seq, D_MODEL)
    cur = lambda s: jnp.minimum(s, n_tiles - 1)
    prev = lambda s: jnp.maximum(s - 1, 0)
    row_cur = pl.BlockSpec((tm, D_MODEL), lambda s: (cur(s), 0))
    row_prev = pl.BlockSpec((tm, D_MODEL), lambda s: (prev(s), 0))
    per_b = lambda shape, tile: pl.BlockSpec(shape, lambda s: (tile(s) // nj, 0, 0))
    consts = [w["g_mix"], w["w_in"], w["wmap"], w["b_map"], w["p_scale"], w["w_dw"], w["b_dw"],
              w["ln_g"], w["ln_b"], w["w_out"], w["g_attn"], w["w_q"], w["w_o"]]
    ffn32 = [ffn_w["w_gate"], ffn_w["w_up"], ffn_w["w_down"]]
    ffn_specs = [pl.BlockSpec((t.shape[0] // n_tiles, t.shape[1]), lambda s: (cur(s), 0)) for t in ffn32]
    return pl.pallas_call(
        functools.partial(_mixer_kernel, tm=tm, nj=nj, n_tiles=n_tiles),
        grid=(n_tiles + 1,),
        in_specs=[row_cur, row_prev, per_b((1, D_MODEL, N_MEM), prev), per_b((1, N_MEM, D_MODEL), prev)]
                 + [_const_spec(c.shape) for c in consts] + ffn_specs,
        out_specs=[row_prev, per_b((1, POOL_HIST, POOL_WIDTH), cur),
                   per_b((1, CONV_HIST, CONV_CH), cur)] + ffn_specs,
        out_shape=[jax.ShapeDtypeStruct((nb * seq, D_MODEL), F32),
                   jax.ShapeDtypeStruct((nb, POOL_HIST, POOL_WIDTH), F32),
                   jax.ShapeDtypeStruct((nb, CONV_HIST, CONV_CH), F32)]
                  + [jax.ShapeDtypeStruct(t.shape, BF16) for t in ffn32],
        scratch_shapes=[pltpu.VMEM((POOL_SLABS, POOL_PAD + tm, LANES), F32),
                        pltpu.VMEM((CONV_SLABS, CONV_PAD + tm, LANES), F32),
                        pltpu.VMEM((tm, CONV_CH), F32),
                        pltpu.VMEM((tm, D_MODEL), BF16)],
        compiler_params=pltpu.CompilerParams(dimension_semantics=("arbitrary",),
                                             vmem_limit_bytes=VMEM_LIMIT),
        name="mixer",
    )(x2d, x2d, kt, vb, *consts, *ffn32)


def _sample_pre_kernel(x_ref, sp_ref, sc_ref, gmix_ref, win_ref, wmap_ref, bmap_ref, pscale_ref,
                       wdw_ref, bdw_ref, lng_ref, lnb_ref, wout_ref, gattn_ref, wq_ref,
                       x1_ref, q_ref, pst_ref, cst_ref):
    n = x1_ref.shape[0]
    x = _load_row_major(x_ref, n)
    h = _rms(x, gmix_ref[...]).astype(BF16)
    u = _dot(h, win_ref[...])
    a = u[:, 2 * CONV_CH:]
    glu = jnp.concatenate([u[:, 2 * c * LANES:(2 * c + 1) * LANES] * jax.nn.sigmoid(u[:, (2 * c + 1) * LANES:2 * (c + 1) * LANES])
                           for c in range(CONV_SLABS)], axis=-1)

    ds = []
    for g, w in enumerate(POOL_WINDOWS):
        c0 = g * POOL_GROUP_W
        a_g = a[:, c0:c0 + POOL_GROUP_W]
        win = a_g
        for i in range(1, w):
            win = win + sp_ref[POOL_HIST - i, :, c0:c0 + POOL_GROUP_W]
        ds.append(win * (1.0 / w) - a_g)
    ya = _pool_map(jnp.concatenate(ds, axis=-1), wmap_ref, bmap_ref[...], pscale_ref[...])

    conv = glu * wdw_ref[CONV_HIST, 0:1, :] + bdw_ref[...]
    for k in range(CONV_HIST):
        conv = conv + sc_ref[k] * wdw_ref[k, 0:1, :]
    yb = _layernorm_silu(conv, lng_ref[...], lnb_ref[...])

    pst_ref[0:POOL_HIST - 1] = sp_ref[1:POOL_HIST]
    pst_ref[POOL_HIST - 1] = a
    cst_ref[0:CONV_HIST - 1] = sc_ref[1:CONV_HIST]
    cst_ref[CONV_HIST - 1] = glu

    mix = jnp.concatenate([ya, yb], axis=-1).astype(BF16)
    x1 = x + _dot(mix, wout_ref[...])
    x1_ref[...] = x1
    q = _dot(_rms(x1, gattn_ref[...]).astype(BF16), wq_ref[...]) * Q_SCALE
    for r in range(HEAD_ROWS):
        c0 = _head_row_cols(r)
        q_ref[pl.ds(r, n, stride=HEAD_ROWS), :] = q[:, c0:c0 + LANES]


def _sample_pre(x, sp, sc, w):
    n = sp.shape[1]
    consts = [w["g_mix"], w["w_in"], w["wmap"], w["b_map"], w["p_scale"], w["w_dw"], w["b_dw"],
              w["ln_g"], w["ln_b"], w["w_out"], w["g_attn"], w["w_q"]]
    args = [x, sp, sc] + consts
    return pl.pallas_call(
        _sample_pre_kernel,
        grid=(1,),
        in_specs=[_const_spec(t.shape) for t in args],
        out_specs=[_const_spec((n, D_MODEL)), _const_spec((n * HEAD_ROWS, LANES)),
                   _const_spec((POOL_HIST, n, POOL_WIDTH)), _const_spec((CONV_HIST, n, CONV_CH))],
        out_shape=[jax.ShapeDtypeStruct((n, D_MODEL), F32), jax.ShapeDtypeStruct((n * HEAD_ROWS, LANES), F32),
                   jax.ShapeDtypeStruct((POOL_HIST, n, POOL_WIDTH), F32),
                   jax.ShapeDtypeStruct((CONV_HIST, n, CONV_CH), F32)],
        compiler_params=pltpu.CompilerParams(dimension_semantics=("arbitrary",),
                                             vmem_limit_bytes=VMEM_LIMIT),
        name="sample_pre",
    )(*args)


def _attend_one(q, k, v):
    kq = k * q[None]
    s = jnp.sum(kq + pltpu.roll(kq, MEM_HEADS, axis=1), axis=-1, keepdims=True)
    e = jnp.exp2(s - jnp.max(s, axis=0, keepdims=True))
    return jnp.sum(e * v, axis=0) / jnp.sum(e, axis=0)


def _ffn(x2, gffn_ref, wg_ref, wu_ref, wd_ref, gfin_ref):
    h = _rms(x2, gffn_ref[...]).astype(BF16)
    act = (_silu(_dot(h, wg_ref[...])) * _dot(h, wu_ref[...])).astype(BF16)
    x3 = x2 + _dot(act, wd_ref[...])
    return _rms(x3, gfin_ref[...])


def _ffn_attn_kernel(x2_ref, gffn_ref, wg_ref, wu_ref, wd_ref, gfin_ref, q_ref, k_ref, v_ref, x1s_ref, wo_ref,
                     y_ref, ys_ref, os_all, *, sb, steps):
    i = pl.program_id(0)
    for t in range(sb):
        row0 = pl.multiple_of((i * sb + t) * HEAD_ROWS, HEAD_ROWS)
        os_all[pl.ds(row0, HEAD_ROWS), :] = _attend_one(q_ref[t], k_ref[t], v_ref[t])
    y_ref[...] = _ffn(x2_ref[...], gffn_ref, wg_ref, wu_ref, wd_ref, gfin_ref)

    @pl.when(i == steps - 1)
    def _():
        n = x1s_ref.shape[0]
        cols = {_head_row_cols(r): os_all[pl.ds(r, n, stride=HEAD_ROWS), :] for r in range(HEAD_ROWS)}
        o = jnp.concatenate([cols[c] for c in sorted(cols)], axis=-1).astype(BF16)
        x2s = x1s_ref[...] + _dot(o, wo_ref[...])
        _store_row_major(ys_ref, _ffn(x2s, gffn_ref, wg_ref, wu_ref, wd_ref, gfin_ref))


def _to_head_rows(t):
    lead = t.shape[:-2]
    t = t.reshape(lead + (MEM_HEADS, 2, LANES))
    return jnp.swapaxes(t, -3, -2).reshape(lead + (HEAD_ROWS, LANES))


def _from_head_rows(t):
    lead = t.shape[:-2]
    t = t.reshape(lead + (2, MEM_HEADS, LANES))
    return jnp.swapaxes(t, -3, -2).reshape(lead + (MEM_HEADS, MEM_HEAD_DIM))


def _ffn_attn(x2, w, tm, q8, k, v, x1s):
    rows = x2.shape[0]
    steps = rows // tm
    n = x1s.shape[0]
    sb = n // steps
    assert sb * steps == n
    row = pl.BlockSpec((tm, D_MODEL), lambda i: (i, 0))
    consts = [w["g_ffn"], w["w_gate"], w["w_up"], w["w_down"], w["g_final"]]
    blk = pl.BlockSpec((sb, N_MEM, HEAD_ROWS, LANES), lambda i: (i, 0, 0, 0))
    vec = pl.BlockSpec((sb, HEAD_ROWS, LANES), lambda i: (i, 0, 0))
    return pl.pallas_call(
        functools.partial(_ffn_attn_kernel, sb=sb, steps=steps),
        grid=(steps,),
        in_specs=[row] + [_const_spec(c.shape) for c in consts] + [vec, blk, blk]
                 + [_const_spec(x1s.shape), _const_spec(w["w_o"].shape)],
        out_specs=[row, _const_spec((n * SUBLANES, LANES))],
        out_shape=[jax.ShapeDtypeStruct((rows, D_MODEL), F32), jax.ShapeDtypeStruct((n * SUBLANES, LANES), F32)],
        scratch_shapes=[pltpu.VMEM((n * HEAD_ROWS, LANES), F32)],
        compiler_params=pltpu.CompilerParams(dimension_semantics=("arbitrary",), vmem_limit_bytes=VMEM_LIMIT),
        name="ffn_attn",
    )(x2, *consts, q8.reshape(n, HEAD_ROWS, LANES), _to_head_rows(k), _to_head_rows(v), x1s, w["w_o"])


def _prep_weights(g_mix, w_in, pool_map_w, pool_map_b, pool_scale, conv_dw_w, conv_dw_b, conv_ln_g,
                  conv_ln_b, w_out, g_attn, g_mem, w_q, w_k, w_v, w_o, g_ffn, w_gate, w_up, w_down,
                  g_final, l):
    vec = lambda v: v.reshape(1, -1)
    return dict(
        g_mix=vec(g_mix[l]), b_map=vec(pool_map_b[l]), p_scale=vec(pool_scale[l]),
        b_dw=vec(conv_dw_b[l]), ln_g=vec(conv_ln_g[l]), ln_b=vec(conv_ln_b[l]),
        g_attn=vec(g_attn[l]), g_mem=vec(g_mem[l]), g_ffn=vec(g_ffn[l]), g_final=vec(g_final))


def kernel(x_prompt, x_sample, mem_prompt, state_pool, state_conv, cache_mem_k, cache_mem_v, g_mix, w_in,
           pool_map_w, pool_map_b, pool_scale, conv_dw_w, conv_dw_b, conv_ln_g, conv_ln_b, w_out, g_attn,
           g_mem, w_q, w_k, w_v, w_o, g_ffn, w_gate, w_up, w_down, g_final):
    assert state_pool.shape[0] == 1, "single-layer trunk"
    nb, seq, _ = x_prompt.shape
    ns = x_sample.shape[0]
    w = _prep_weights(g_mix, w_in, pool_map_w, pool_map_b, pool_scale, conv_dw_w, conv_dw_b, conv_ln_g,
                      conv_ln_b, w_out, g_attn, g_mem, w_q, w_k, w_v, w_o, g_ffn, w_gate, w_up, w_down,
                      g_final, 0)

    mk, mv, kt, vb, w["wmap"], w["w_dw"], w["w_in"], w["w_out"], w["w_q"], w["w_o"] = _mem_kv(
        mem_prompt, w["g_mem"], w_k[0], w_v[0], pool_map_w[0], conv_dw_w[0], [w_in[0], w_out[0], w_q[0], w_o[0]])

    x1s, qs, pool_s, conv_s = _sample_pre(x_sample.reshape(ns * SUBLANES, LANES), jnp.swapaxes(state_pool[0], 0, 1),
                                          jnp.swapaxes(state_conv[0], 0, 1), w)
    pool_s = jnp.swapaxes(pool_s, 0, 1)
    conv_s = jnp.swapaxes(conv_s, 0, 1)

    ffn_f32 = dict(w_gate=w_gate[0], w_up=w_up[0], w_down=w_down[0])
    x2p, pool_p, conv_p, w["w_gate"], w["w_up"], w["w_down"] = _mixer(x_prompt, kt, vb, w, ffn_f32, tm=1024)
    yp, ys = _ffn_attn(x2p, w, 512, qs, cache_mem_k[0], cache_mem_v[0], x1s)

    return (yp.reshape(nb, seq, D_MODEL), ys.reshape(ns, 1, D_MODEL),
            pool_p[None], pool_s[None], conv_p[None], conv_s[None],
            _from_head_rows(mk.reshape(nb, N_MEM, HEAD_ROWS, LANES))[None],
            _from_head_rows(mv.reshape(nb, N_MEM, HEAD_ROWS, LANES))[None])
```

```python
import functools
import math

import jax
import jax.numpy as jnp
from jax import lax
from jax.experimental import pallas as pl
from jax.experimental.pallas import tpu as pltpu

D_MODEL = 1024
POOL_WINDOWS = (2, 4, 8, 16)
POOL_GROUP_W = 128
POOL_WIDTH = 512
POOL_HIST = 15
CONV_CH = 512
CONV_K = 31
CONV_HIST = 30
IN_COLS = POOL_WIDTH + 2 * CONV_CH
N_MEM = 256
MEM_HEADS = 4
MEM_HEAD_DIM = 256
EPS = 1e-6
ATTN_SCALE = 1.0 / math.sqrt(MEM_HEAD_DIM)
Q_SCALE = ATTN_SCALE * math.log2(math.e)

SUBLANES = 8
LANES = 128
POOL_SLABS = POOL_WIDTH // LANES
CONV_SLABS = CONV_CH // LANES
HEAD_ROWS = 2 * MEM_HEADS
MIXER_ROWS = 1024
FFN_ROWS = 512
MEM_BATCH_ROWS = 2
SAMPLE_CHUNKS = 2
CONV_ROWS = 32
POOL_PAD = 16
CONV_PAD = 32
VMEM_LIMIT = 60 * 1024 * 1024

BF16 = jnp.bfloat16
F32 = jnp.float32


def _rms(x, g):
    ms = jnp.mean(x * x, axis=-1, keepdims=True)
    return x * lax.rsqrt(ms + EPS) * g


def _dot(a, b):
    return jnp.dot(a, b, preferred_element_type=F32)


def _silu(x):
    return x * jax.nn.sigmoid(x)


def _layernorm_silu(y, g, b):
    mu = jnp.mean(y, axis=-1, keepdims=True)
    yc = y - mu
    var = jnp.mean(yc * yc, axis=-1, keepdims=True)
    return _silu(yc * lax.rsqrt(var + EPS) * g + b)


def _pool_map(d, wmap_ref, bmap, pscale):
    db = d.astype(BF16)
    y = jnp.concatenate([_dot(db[:, :256], wmap_ref[0]), _dot(db[:, 256:], wmap_ref[1])], axis=-1)
    return (y + bmap) * pscale


def _load_row_major(ref, n):
    return jnp.concatenate([ref[pl.ds(j, n, stride=SUBLANES), :] for j in range(D_MODEL // LANES)], axis=-1)


def _store_row_major(ref, val):
    for j in range(D_MODEL // LANES):
        ref[pl.ds(j, val.shape[0], stride=SUBLANES), :] = val[:, j * LANES:(j + 1) * LANES]


def _head_row_cols(r):
    half, head = divmod(r, MEM_HEADS)
    return head * MEM_HEAD_DIM + half * LANES


def _const_spec(shape):
    nd = len(shape)
    return pl.BlockSpec(shape, lambda *_: (0,) * nd, pipeline_mode=pl.Buffered(1))


def _mem_kv_kernel(mem_ref, g_ref, wk32_ref, wv32_ref, pm_ref, wdw_ref, win32_ref, wout32_ref, wq32_ref, wo32_ref,
                   k_ref, v_ref, kt_ref, vb_ref, wmap_ref, wdw8_ref, win16_ref, wout16_ref, wq16_ref, wo16_ref,
                   wk16, wv16):
    @pl.when(pl.program_id(0) == 0)
    def _():
        wk16[...] = wk32_ref[...].astype(BF16)
        wv16[...] = wv32_ref[...].astype(BF16)
        wmap_ref[...] = jnp.zeros(wmap_ref.shape, BF16)
        for g in range(len(POOL_WINDOWS)):
            d0 = (g % 2) * POOL_GROUP_W
            wmap_ref[g // 2, d0:d0 + POOL_GROUP_W, d0:d0 + POOL_GROUP_W] = pm_ref[g].astype(BF16)
        for k in range(CONV_K):
            wdw8_ref[k] = jnp.broadcast_to(wdw_ref[k:k + 1, :], (SUBLANES, CONV_CH))

    for c in range(CONV_SLABS):
        for o, src0 in enumerate((POOL_WIDTH, POOL_WIDTH + CONV_CH)):
            win16_ref[:, (2 * c + o) * LANES:(2 * c + o + 1) * LANES] = (
                win32_ref[:, src0 + c * LANES:src0 + (c + 1) * LANES].astype(BF16))
    win16_ref[:, 2 * CONV_CH:] = win32_ref[:, :POOL_WIDTH].astype(BF16)
    wout16_ref[...] = wout32_ref[...].astype(BF16)
    wq16_ref[...] = wq32_ref[...].astype(BF16)
    wo16_ref[...] = wo32_ref[...].astype(BF16)

    nbb = mem_ref.shape[0]
    m = _rms(mem_ref[...].reshape(nbb * N_MEM, D_MODEL), g_ref[...]).astype(BF16)
    k = _dot(m, wk16[...])
    v = _dot(m, wv16[...])
    for b in range(nbb):
        kb = k[b * N_MEM:(b + 1) * N_MEM]
        vb = v[b * N_MEM:(b + 1) * N_MEM]
        for r in range(HEAD_ROWS):
            c0 = _head_row_cols(r)
            k_ref[b, pl.ds(r, N_MEM, stride=HEAD_ROWS), :] = kb[:, c0:c0 + LANES]
            v_ref[b, pl.ds(r, N_MEM, stride=HEAD_ROWS), :] = vb[:, c0:c0 + LANES]
        kt_ref[b] = kb.T.astype(BF16)
        vb_ref[b] = vb.astype(BF16)


def _mem_kv(mem, g_mem, wk, wv, pool_map, conv_taps, side, nbb):
    nb = mem.shape[0]
    steps = nb // nbb
    blk = lambda shape: pl.BlockSpec(shape, lambda b: (b, 0, 0))
    rows_blk = blk((nbb, N_MEM * HEAD_ROWS, LANES))
    side_specs = [pl.BlockSpec((t.shape[0] // steps, t.shape[1]), lambda b: (b, 0)) for t in side]
    wmap_shape = (len(POOL_WINDOWS) // 2, 2 * POOL_GROUP_W, 2 * POOL_GROUP_W)
    wdw8_shape = (CONV_K, SUBLANES, CONV_CH)
    return pl.pallas_call(
        _mem_kv_kernel,
        grid=(steps,),
        in_specs=[blk((nbb, N_MEM, D_MODEL)), _const_spec((1, D_MODEL)),
                  _const_spec((D_MODEL, D_MODEL)), _const_spec((D_MODEL, D_MODEL)),
                  _const_spec(pool_map.shape), _const_spec(conv_taps.shape)] + side_specs,
        out_specs=[rows_blk, rows_blk, blk((nbb, D_MODEL, N_MEM)), blk((nbb, N_MEM, D_MODEL)),
                   _const_spec(wmap_shape), _const_spec(wdw8_shape)] + side_specs,
        out_shape=[jax.ShapeDtypeStruct((nb, N_MEM * HEAD_ROWS, LANES), F32),
                   jax.ShapeDtypeStruct((nb, N_MEM * HEAD_ROWS, LANES), F32),
                   jax.ShapeDtypeStruct((nb, D_MODEL, N_MEM), BF16),
                   jax.ShapeDtypeStruct((nb, N_MEM, D_MODEL), BF16),
                   jax.ShapeDtypeStruct(wmap_shape, BF16), jax.ShapeDtypeStruct(wdw8_shape, F32)]
                  + [jax.ShapeDtypeStruct(t.shape, BF16) for t in side],
        scratch_shapes=[pltpu.VMEM((D_MODEL, D_MODEL), BF16), pltpu.VMEM((D_MODEL, D_MODEL), BF16)],
        compiler_params=pltpu.CompilerParams(dimension_semantics=("arbitrary",),
                                             vmem_limit_bytes=VMEM_LIMIT),
        name="mem_kv",
    )(mem, g_mem, wk, wv, pool_map, conv_taps, *side)


def _mixer_kernel(x_ref, xres_ref, kt_ref, vb_ref, gmix_ref, win_ref, wmap_ref, bmap_ref, pscale_ref,
                  wdw_ref, bdw_ref, lng_ref, lnb_ref, wout_ref, gattn_ref, wq_ref, wo_ref,
                  wg32_ref, wu32_ref, wd32_ref,
                  x2_ref, pst_ref, cst_ref, wg16_ref, wu16_ref, wd16_ref,
                  a_ext, g_ext, conv_buf, mix_buf, *, tm, nj, n_tiles):
    s = pl.program_id(0)
    j = s % nj

    @pl.when(s == 0)
    def _():
        mix_buf[...] = jnp.zeros(mix_buf.shape, BF16)

    @pl.when(j == 0)
    def _():
        a_ext[:, 0:POOL_PAD, :] = jnp.zeros((POOL_SLABS, POOL_PAD, LANES), F32)
        g_ext[:, 0:CONV_PAD, :] = jnp.zeros((CONV_SLABS, CONV_PAD, LANES), F32)

    def project_and_attend():
        x1 = xres_ref[...] + _dot(mix_buf[...], wout_ref[...])
        q = (_dot(_rms(x1, gattn_ref[...]).astype(BF16), wq_ref[...]) * Q_SCALE).astype(BF16)
        heads = []
        for hd in range(MEM_HEADS):
            c0 = hd * MEM_HEAD_DIM
            sc = _dot(q[:, c0:c0 + MEM_HEAD_DIM], kt_ref[0, c0:c0 + MEM_HEAD_DIM, :])
            e = jnp.exp2(sc - jnp.max(sc, axis=-1, keepdims=True))
            p = (e * (1.0 / jnp.sum(e, axis=-1, keepdims=True))).astype(BF16)
            heads.append(_dot(p, vb_ref[0, :, c0:c0 + MEM_HEAD_DIM]).astype(BF16))
        x2_ref[...] = x1 + _dot(jnp.concatenate(heads, axis=-1), wo_ref[...])

    @pl.when(s < n_tiles)
    def _():
        wg16_ref[...] = wg32_ref[...].astype(BF16)
        wu16_ref[...] = wu32_ref[...].astype(BF16)
        wd16_ref[...] = wd32_ref[...].astype(BF16)

        h = _rms(x_ref[...], gmix_ref[...]).astype(BF16)
        for c in range(CONV_SLABS):
            u = _dot(h, win_ref[:, 2 * c * LANES:2 * (c + 1) * LANES])
            g_ext[c, CONV_PAD:CONV_PAD + tm, :] = u[:, :LANES] * jax.nn.sigmoid(u[:, LANES:])
        u = _dot(h, win_ref[:, 2 * CONV_CH:])
        for c in range(POOL_SLABS):
            a_ext[c, POOL_PAD:POOL_PAD + tm, :] = u[:, c * LANES:(c + 1) * LANES]

        project_and_attend()

        pos = j * tm + lax.broadcasted_iota(jnp.int32, (tm, 1), 0)
        ds = []
        for g, w in enumerate(POOL_WINDOWS):
            a_g = a_ext[g, POOL_PAD:POOL_PAD + tm, :]
            win = a_g
            for i in range(1, w):
                win = win + a_ext[g, POOL_PAD - i:POOL_PAD - i + tm, :]
            inv_cnt = 1.0 / jnp.minimum(w, pos + 1).astype(F32)
            ds.append(win * inv_cnt - a_g)
        ya = _pool_map(jnp.concatenate(ds, axis=-1), wmap_ref, bmap_ref[...], pscale_ref[...])

        for c in range(CONV_SLABS):
            for t0 in range(0, tm, CONV_ROWS):
                acc = None
                for k in range(CONV_K):
                    r0 = CONV_PAD - CONV_HIST + t0 + k
                    g = g_ext[c, r0:r0 + CONV_ROWS, :].reshape(CONV_ROWS // SUBLANES, SUBLANES, LANES)
                    term = g * wdw_ref[k, :, c * LANES:(c + 1) * LANES][None]
                    acc = term if acc is None else acc + term
                conv_buf[t0:t0 + CONV_ROWS, c * LANES:(c + 1) * LANES] = acc.reshape(CONV_ROWS, LANES)
        yb = _layernorm_silu(conv_buf[...] + bdw_ref[...], lng_ref[...], lnb_ref[...])
        mix_buf[...] = jnp.concatenate([ya, yb], axis=-1).astype(BF16)

        for c in range(POOL_SLABS):
            pst_ref[0, :, c * LANES:(c + 1) * LANES] = a_ext[c, POOL_PAD + tm - POOL_HIST:POOL_PAD + tm, :]
            a_ext[c, 0:POOL_PAD, :] = a_ext[c, tm:tm + POOL_PAD, :]
        for c in range(CONV_SLABS):
            cst_ref[0, :, c * LANES:(c + 1) * LANES] = g_ext[c, CONV_PAD + tm - CONV_HIST:CONV_PAD + tm, :]
            g_ext[c, 0:CONV_PAD, :] = g_ext[c, tm:tm + CONV_PAD, :]

    @pl.when(s == n_tiles)
    def _():
        project_and_attend()


def _mixer(x, kt, vb, w, ffn_w, tm):
    nb, seq, _ = x.shape
    nj = seq // tm
    n_tiles = nb * nj
    x2d = x.reshape(nb * seq, D_MODEL)
    cur = lambda s: jnp.minimum(s, n_tiles - 1)
    prev = lambda s: jnp.maximum(s - 1, 0)
    row_cur = pl.BlockSpec((tm, D_MODEL), lambda s: (cur(s), 0))
    row_prev = pl.BlockSpec((tm, D_MODEL), lambda s: (prev(s), 0))
    per_b = lambda shape, tile: pl.BlockSpec(shape, lambda s: (tile(s) // nj, 0, 0))
    consts = [w["g_mix"], w["w_in"], w["wmap"], w["b_map"], w["p_scale"], w["w_dw"], w["b_dw"],
              w["ln_g"], w["ln_b"], w["w_out"], w["g_attn"], w["w_q"], w["w_o"]]
    ffn32 = [ffn_w["w_gate"], ffn_w["w_up"], ffn_w["w_down"]]
    ffn_specs = [pl.BlockSpec((t.shape[0] // n_tiles, t.shape[1]), lambda s: (cur(s), 0)) for t in ffn32]
    return pl.pallas_call(
        functools.partial(_mixer_kernel, tm=tm, nj=nj, n_tiles=n_tiles),
        grid=(n_tiles + 1,),
        in_specs=[row_cur, row_prev, per_b((1, D_MODEL, N_MEM), prev), per_b((1, N_MEM, D_MODEL), prev)]
                 + [_const_spec(c.shape) for c in consts] + ffn_specs,
        out_specs=[row_prev, per_b((1, POOL_HIST, POOL_WIDTH), cur),
                   per_b((1, CONV_HIST, CONV_CH), cur)] + ffn_specs,
        out_shape=[jax.ShapeDtypeStruct((nb * seq, D_MODEL), F32),
                   jax.ShapeDtypeStruct((nb, POOL_HIST, POOL_WIDTH), F32),
                   jax.ShapeDtypeStruct((nb, CONV_HIST, CONV_CH), F32)]
                  + [jax.ShapeDtypeStruct(t.shape, BF16) for t in ffn32],
        scratch_shapes=[pltpu.VMEM((POOL_SLABS, POOL_PAD + tm, LANES), F32),
                        pltpu.VMEM((CONV_SLABS, CONV_PAD + tm, LANES), F32),
                        pltpu.VMEM((tm, CONV_CH), F32),
                        pltpu.VMEM((tm, D_MODEL), BF16)],
        compiler_params=pltpu.CompilerParams(dimension_semantics=("arbitrary",),
                                             vmem_limit_bytes=VMEM_LIMIT),
        name="mixer",
    )(x2d, x2d, kt, vb, *consts, *ffn32)


def _sample_pre_kernel(x_ref, sp_ref, sc_ref, gmix_ref, win_ref, wmap_ref, bmap_ref, pscale_ref,
                       wdw_ref, bdw_ref, lng_ref, lnb_ref, wout_ref, gattn_ref, wq_ref,
                       x1_ref, q_ref, pst_ref, cst_ref):
    n = x1_ref.shape[0]
    x = _load_row_major(x_ref, n)
    h = _rms(x, gmix_ref[...]).astype(BF16)
    u = _dot(h, win_ref[...])
    a = u[:, 2 * CONV_CH:]
    glu = jnp.concatenate([u[:, 2 * c * LANES:(2 * c + 1) * LANES] * jax.nn.sigmoid(u[:, (2 * c + 1) * LANES:2 * (c + 1) * LANES])
                           for c in range(CONV_SLABS)], axis=-1)

    ds = []
    for g, w in enumerate(POOL_WINDOWS):
        c0 = g * POOL_GROUP_W
        a_g = a[:, c0:c0 + POOL_GROUP_W]
        win = a_g
        for i in range(1, w):
            win = win + sp_ref[POOL_HIST - i, :, c0:c0 + POOL_GROUP_W]
        ds.append(win * (1.0 / w) - a_g)
    ya = _pool_map(jnp.concatenate(ds, axis=-1), wmap_ref, bmap_ref[...], pscale_ref[...])

    conv = glu * wdw_ref[CONV_HIST, 0:1, :] + bdw_ref[...]
    for k in range(CONV_HIST):
        conv = conv + sc_ref[k] * wdw_ref[k, 0:1, :]
    yb = _layernorm_silu(conv, lng_ref[...], lnb_ref[...])

    pst_ref[0:POOL_HIST - 1] = sp_ref[1:POOL_HIST]
    pst_ref[POOL_HIST - 1] = a
    cst_ref[0:CONV_HIST - 1] = sc_ref[1:CONV_HIST]
    cst_ref[CONV_HIST - 1] = glu

    mix = jnp.concatenate([ya, yb], axis=-1).astype(BF16)
    x1 = x + _dot(mix, wout_ref[...])
    x1_ref[...] = x1
    q = _dot(_rms(x1, gattn_ref[...]).astype(BF16), wq_ref[...]) * Q_SCALE
    for r in range(HEAD_ROWS):
        c0 = _head_row_cols(r)
        q_ref[pl.ds(r, n, stride=HEAD_ROWS), :] = q[:, c0:c0 + LANES]


def _sample_pre(x, sp, sc, w):
    n = sp.shape[1]
    nc = n // SAMPLE_CHUNKS
    consts = [w["g_mix"], w["w_in"], w["wmap"], w["b_map"], w["p_scale"], w["w_dw"], w["b_dw"],
              w["ln_g"], w["ln_b"], w["w_out"], w["g_attn"], w["w_q"]]
    rows = lambda r, width: pl.BlockSpec((r, width), lambda i: (i, 0))
    hist = lambda h, width: pl.BlockSpec((h, nc, width), lambda i: (0, i, 0))
    return pl.pallas_call(
        _sample_pre_kernel,
        grid=(SAMPLE_CHUNKS,),
        in_specs=[rows(nc * SUBLANES, LANES), hist(POOL_HIST, POOL_WIDTH), hist(CONV_HIST, CONV_CH)]
                 + [_const_spec(t.shape) for t in consts],
        out_specs=[rows(nc, D_MODEL), rows(nc * HEAD_ROWS, LANES),
                   hist(POOL_HIST, POOL_WIDTH), hist(CONV_HIST, CONV_CH)],
        out_shape=[jax.ShapeDtypeStruct((n, D_MODEL), F32), jax.ShapeDtypeStruct((n * HEAD_ROWS, LANES), F32),
                   jax.ShapeDtypeStruct((POOL_HIST, n, POOL_WIDTH), F32),
                   jax.ShapeDtypeStruct((CONV_HIST, n, CONV_CH), F32)],
        compiler_params=pltpu.CompilerParams(dimension_semantics=("arbitrary",),
                                             vmem_limit_bytes=VMEM_LIMIT),
        name="sample_pre",
    )(x, sp, sc, *consts)


def _attend_one(q, k, v):
    kq = k * q[None]
    s = jnp.sum(kq + pltpu.roll(kq, MEM_HEADS, axis=1), axis=-1, keepdims=True)
    e = jnp.exp2(s - jnp.max(s, axis=0, keepdims=True))
    return jnp.sum(e * v, axis=0) / jnp.sum(e, axis=0)


def _ffn(x2, gffn_ref, wg_ref, wu_ref, wd_ref, gfin_ref):
    h = _rms(x2, gffn_ref[...]).astype(BF16)
    act = (_silu(_dot(h, wg_ref[...])) * _dot(h, wu_ref[...])).astype(BF16)
    x3 = x2 + _dot(act, wd_ref[...])
    return _rms(x3, gfin_ref[...])


def _ffn_attn_kernel(x2_ref, gffn_ref, wg_ref, wu_ref, wd_ref, gfin_ref, q_ref, k_ref, v_ref, x1s_ref, wo_ref,
                     y_ref, ys_ref, os_all, *, sb, steps):
    i = pl.program_id(0)
    for t in range(sb):
        row0 = pl.multiple_of((i * sb + t) * HEAD_ROWS, HEAD_ROWS)
        os_all[pl.ds(row0, HEAD_ROWS), :] = _attend_one(q_ref[t], k_ref[t], v_ref[t])
    y_ref[...] = _ffn(x2_ref[...], gffn_ref, wg_ref, wu_ref, wd_ref, gfin_ref)

    @pl.when(i == steps - 1)
    def _():
        n = x1s_ref.shape[0]
        cols = {_head_row_cols(r): os_all[pl.ds(r, n, stride=HEAD_ROWS), :] for r in range(HEAD_ROWS)}
        o = jnp.concatenate([cols[c] for c in sorted(cols)], axis=-1).astype(BF16)
        x2s = x1s_ref[...] + _dot(o, wo_ref[...])
        _store_row_major(ys_ref, _ffn(x2s, gffn_ref, wg_ref, wu_ref, wd_ref, gfin_ref))


def _to_head_rows(t):
    lead = t.shape[:-2]
    t = t.reshape(lead + (MEM_HEADS, 2, LANES))
    return jnp.swapaxes(t, -3, -2).reshape(lead + (HEAD_ROWS, LANES))


def _from_head_rows(t):
    lead = t.shape[:-2]
    t = t.reshape(lead + (2, MEM_HEADS, LANES))
    return jnp.swapaxes(t, -3, -2).reshape(lead + (MEM_HEADS, MEM_HEAD_DIM))


def _ffn_attn(x2, w, tm, q8, k, v, x1s):
    rows = x2.shape[0]
    steps = rows // tm
    n = x1s.shape[0]
    sb = n // steps
    assert sb * steps == n
    row = pl.BlockSpec((tm, D_MODEL), lambda i: (i, 0))
    consts = [w["g_ffn"], w["w_gate"], w["w_up"], w["w_down"], w["g_final"]]
    blk = pl.BlockSpec((sb, N_MEM, HEAD_ROWS, LANES), lambda i: (i, 0, 0, 0))
    vec = pl.BlockSpec((sb, HEAD_ROWS, LANES), lambda i: (i, 0, 0))
    return pl.pallas_call(
        functools.partial(_ffn_attn_kernel, sb=sb, steps=steps),
        grid=(steps,),
        in_specs=[row] + [_const_spec(c.shape) for c in consts] + [vec, blk, blk]
                 + [_const_spec(x1s.shape), _const_spec(w["w_o"].shape)],
        out_specs=[row, _const_spec((n * SUBLANES, LANES))],
        out_shape=[jax.ShapeDtypeStruct((rows, D_MODEL), F32), jax.ShapeDtypeStruct((n * SUBLANES, LANES), F32)],
        scratch_shapes=[pltpu.VMEM((n * HEAD_ROWS, LANES), F32)],
        compiler_params=pltpu.CompilerParams(dimension_semantics=("arbitrary",), vmem_limit_bytes=VMEM_LIMIT),
        name="ffn_attn",
    )(x2, *consts, q8.reshape(n, HEAD_ROWS, LANES), _to_head_rows(k), _to_head_rows(v), x1s, w["w_o"])


def _prep_weights(g_mix, w_in, pool_map_w, pool_map_b, pool_scale, conv_dw_w, conv_dw_b, conv_ln_g,
                  conv_ln_b, w_out, g_attn, g_mem, w_q, w_k, w_v, w_o, g_ffn, w_gate, w_up, w_down,
                  g_final, l):
    vec = lambda v: v.reshape(1, -1)
    return dict(
        g_mix=vec(g_mix[l]), b_map=vec(pool_map_b[l]), p_scale=vec(pool_scale[l]),
        b_dw=vec(conv_dw_b[l]), ln_g=vec(conv_ln_g[l]), ln_b=vec(conv_ln_b[l]),
        g_attn=vec(g_attn[l]), g_mem=vec(g_mem[l]), g_ffn=vec(g_ffn[l]), g_final=vec(g_final))


def kernel(x_prompt, x_sample, mem_prompt, state_pool, state_conv, cache_mem_k, cache_mem_v, g_mix, w_in,
           pool_map_w, pool_map_b, pool_scale, conv_dw_w, conv_dw_b, conv_ln_g, conv_ln_b, w_out, g_attn,
           g_mem, w_q, w_k, w_v, w_o, g_ffn, w_gate, w_up, w_down, g_final):
    assert state_pool.shape[0] == 1, "single-layer trunk"
    nb, seq, _ = x_prompt.shape
    ns = x_sample.shape[0]
    w = _prep_weights(g_mix, w_in, pool_map_w, pool_map_b, pool_scale, conv_dw_w, conv_dw_b, conv_ln_g,
                      conv_ln_b, w_out, g_attn, g_mem, w_q, w_k, w_v, w_o, g_ffn, w_gate, w_up, w_down,
                      g_final, 0)

    mk, mv, kt, vb, w["wmap"], w["w_dw"], w["w_in"], w["w_out"], w["w_q"], w["w_o"] = _mem_kv(
        mem_prompt, w["g_mem"], w_k[0], w_v[0], pool_map_w[0], conv_dw_w[0], [w_in[0], w_out[0], w_q[0], w_o[0]],
        MEM_BATCH_ROWS)

    x1s, qs, pool_s, conv_s = _sample_pre(x_sample.reshape(ns * SUBLANES, LANES), jnp.swapaxes(state_pool[0], 0, 1),
                                          jnp.swapaxes(state_conv[0], 0, 1), w)
    pool_s = jnp.swapaxes(pool_s, 0, 1)
    conv_s = jnp.swapaxes(conv_s, 0, 1)

    ffn_f32 = dict(w_gate=w_gate[0], w_up=w_up[0], w_down=w_down[0])
    x2p, pool_p, conv_p, w["w_gate"], w["w_up"], w["w_down"] = _mixer(x_prompt, kt, vb, w, ffn_f32, MIXER_ROWS)
    yp, ys = _ffn_attn(x2p, w, FFN_ROWS, qs, cache_mem_k[0], cache_mem_v[0], x1s)

    return (yp.reshape(nb, seq, D_MODEL), ys.reshape(ns, 1, D_MODEL),
            pool_p[None], pool_s[None], conv_p[None], conv_s[None],
            _from_head_rows(mk.reshape(nb, N_MEM, HEAD_ROWS, LANES))[None],
            _from_head_rows(mv.reshape(nb, N_MEM, HEAD_ROWS, LANES))[None])
```

```python
import functools
import math

import jax
import jax.numpy as jnp
from jax import lax
from jax.experimental import pallas as pl
from jax.experimental.pallas import tpu as pltpu

D_MODEL = 1024
POOL_WINDOWS = (2, 4, 8, 16)
POOL_GROUP_W = 128
POOL_WIDTH = 512
POOL_HIST = 15
CONV_CH = 512
CONV_K = 31
CONV_HIST = 30
IN_COLS = POOL_WIDTH + 2 * CONV_CH
N_MEM = 256
MEM_HEADS = 4
MEM_HEAD_DIM = 256
EPS = 1e-6
ATTN_SCALE = 1.0 / math.sqrt(MEM_HEAD_DIM)
Q_SCALE = ATTN_SCALE * math.log2(math.e)

SUBLANES = 8
LANES = 128
POOL_SLABS = POOL_WIDTH // LANES
CONV_SLABS = CONV_CH // LANES
HEAD_ROWS = 2 * MEM_HEADS
MIXER_ROWS = 1024
FFN_ROWS = 512
MEM_BATCH_ROWS = 1
SAMPLE_CHUNKS = 2
CONV_ROWS = 32
POOL_PAD = 16
CONV_PAD = 32
VMEM_LIMIT = 62 * 1024 * 1024

BF16 = jnp.bfloat16
F32 = jnp.float32


def _rms(x, g):
    ms = jnp.mean(x * x, axis=-1, keepdims=True)
    return x * lax.rsqrt(ms + EPS) * g


def _dot(a, b):
    return jnp.dot(a, b, preferred_element_type=F32)


def _silu(x):
    return x * jax.nn.sigmoid(x)


def _layernorm_silu(y, g, b):
    mu = jnp.mean(y, axis=-1, keepdims=True)
    yc = y - mu
    var = jnp.mean(yc * yc, axis=-1, keepdims=True)
    return _silu(yc * lax.rsqrt(var + EPS) * g + b)


def _pool_map(d, wmap_ref, bmap, pscale):
    db = d.astype(BF16)
    y = jnp.concatenate([_dot(db[:, :256], wmap_ref[0]), _dot(db[:, 256:], wmap_ref[1])], axis=-1)
    return (y + bmap) * pscale


def _load_row_major(ref, n):
    return jnp.concatenate([ref[pl.ds(j, n, stride=SUBLANES), :] for j in range(D_MODEL // LANES)], axis=-1)


def _store_row_major(ref, val):
    for j in range(D_MODEL // LANES):
        ref[pl.ds(j, val.shape[0], stride=SUBLANES), :] = val[:, j * LANES:(j + 1) * LANES]


def _head_row_cols(r):
    half, head = divmod(r, MEM_HEADS)
    return head * MEM_HEAD_DIM + half * LANES


def _const_spec(shape):
    nd = len(shape)
    return pl.BlockSpec(shape, lambda *_: (0,) * nd, pipeline_mode=pl.Buffered(1))


def _mem_kv_kernel(mem_ref, g_ref, wk32_ref, wv32_ref, wq32_ref, wo32_ref, pm_ref, wdw_ref, win32_ref, wout32_ref,
                   k_ref, v_ref, qk_ref, vo_ref, wmap_ref, wdw8_ref, wq16_ref, wo16_ref, win16_ref, wout16_ref,
                   wk16, wv16):
    @pl.when(pl.program_id(0) == 0)
    def _():
        wk16[...] = wk32_ref[...].astype(BF16)
        wv16[...] = wv32_ref[...].astype(BF16)
        wq16_ref[...] = wq32_ref[...].astype(BF16)
        wo16_ref[...] = wo32_ref[...].astype(BF16)
        wmap_ref[...] = jnp.zeros(wmap_ref.shape, BF16)
        for g in range(len(POOL_WINDOWS)):
            d0 = (g % 2) * POOL_GROUP_W
            wmap_ref[g // 2, d0:d0 + POOL_GROUP_W, d0:d0 + POOL_GROUP_W] = pm_ref[g].astype(BF16)
        for k in range(CONV_K):
            wdw8_ref[k] = jnp.broadcast_to(wdw_ref[k:k + 1, :], (SUBLANES, CONV_CH))

    for c in range(CONV_SLABS):
        for o, src0 in enumerate((POOL_WIDTH, POOL_WIDTH + CONV_CH)):
            win16_ref[:, (2 * c + o) * LANES:(2 * c + o + 1) * LANES] = (
                win32_ref[:, src0 + c * LANES:src0 + (c + 1) * LANES].astype(BF16))
    win16_ref[:, 2 * CONV_CH:] = win32_ref[:, :POOL_WIDTH].astype(BF16)
    wout16_ref[...] = wout32_ref[...].astype(BF16)

    nbb = mem_ref.shape[0]
    m = _rms(mem_ref[...].reshape(nbb * N_MEM, D_MODEL), g_ref[...]).astype(BF16)
    k = _dot(m, wk16[...])
    v = _dot(m, wv16[...])
    for b in range(nbb):
        kb = k[b * N_MEM:(b + 1) * N_MEM]
        vb = v[b * N_MEM:(b + 1) * N_MEM]
        for r in range(HEAD_ROWS):
            c0 = _head_row_cols(r)
            k_ref[b, pl.ds(r, N_MEM, stride=HEAD_ROWS), :] = kb[:, c0:c0 + LANES]
            v_ref[b, pl.ds(r, N_MEM, stride=HEAD_ROWS), :] = vb[:, c0:c0 + LANES]
        kt = kb.T.astype(BF16)
        vb16 = vb.astype(BF16)
        for hd in range(MEM_HEADS):
            c0 = hd * MEM_HEAD_DIM
            qk = _dot(wq16_ref[:, c0:c0 + MEM_HEAD_DIM], kt[c0:c0 + MEM_HEAD_DIM, :]) * Q_SCALE
            qk_ref[b, :, hd * N_MEM:(hd + 1) * N_MEM] = qk.astype(BF16)
            vo = _dot(vb16[:, c0:c0 + MEM_HEAD_DIM], wo16_ref[c0:c0 + MEM_HEAD_DIM, :])
            vo_ref[b, hd * N_MEM:(hd + 1) * N_MEM, :] = vo.astype(BF16)


def _mem_kv(mem, g_mem, wk, wv, wq, wo, pool_map, conv_taps, side, nbb):
    nb = mem.shape[0]
    steps = nb // nbb
    blk = lambda shape: pl.BlockSpec(shape, lambda b: (b, 0, 0))
    rows_blk = blk((nbb, N_MEM * HEAD_ROWS, LANES))
    side_specs = [pl.BlockSpec((t.shape[0] // steps, t.shape[1]), lambda b: (b, 0)) for t in side]
    wmap_shape = (len(POOL_WINDOWS) // 2, 2 * POOL_GROUP_W, 2 * POOL_GROUP_W)
    wdw8_shape = (CONV_K, SUBLANES, CONV_CH)
    return pl.pallas_call(
        _mem_kv_kernel,
        grid=(steps,),
        in_specs=[blk((nbb, N_MEM, D_MODEL)), _const_spec((1, D_MODEL)),
                  _const_spec((D_MODEL, D_MODEL)), _const_spec((D_MODEL, D_MODEL)),
                  _const_spec((D_MODEL, D_MODEL)), _const_spec((D_MODEL, D_MODEL)),
                  _const_spec(pool_map.shape), _const_spec(conv_taps.shape)] + side_specs,
        out_specs=[rows_blk, rows_blk, blk((nbb, D_MODEL, MEM_HEADS * N_MEM)), blk((nbb, MEM_HEADS * N_MEM, D_MODEL)),
                   _const_spec(wmap_shape), _const_spec(wdw8_shape),
                   _const_spec((D_MODEL, D_MODEL)), _const_spec((D_MODEL, D_MODEL))] + side_specs,
        out_shape=[jax.ShapeDtypeStruct((nb, N_MEM * HEAD_ROWS, LANES), F32),
                   jax.ShapeDtypeStruct((nb, N_MEM * HEAD_ROWS, LANES), F32),
                   jax.ShapeDtypeStruct((nb, D_MODEL, MEM_HEADS * N_MEM), BF16),
                   jax.ShapeDtypeStruct((nb, MEM_HEADS * N_MEM, D_MODEL), BF16),
                   jax.ShapeDtypeStruct(wmap_shape, BF16), jax.ShapeDtypeStruct(wdw8_shape, F32),
                   jax.ShapeDtypeStruct((D_MODEL, D_MODEL), BF16), jax.ShapeDtypeStruct((D_MODEL, D_MODEL), BF16)]
                  + [jax.ShapeDtypeStruct(t.shape, BF16) for t in side],
        scratch_shapes=[pltpu.VMEM((D_MODEL, D_MODEL), BF16), pltpu.VMEM((D_MODEL, D_MODEL), BF16)],
        compiler_params=pltpu.CompilerParams(dimension_semantics=("arbitrary",),
                                             vmem_limit_bytes=VMEM_LIMIT),
        name="mem_kv",
    )(mem, g_mem, wk, wv, wq, wo, pool_map, conv_taps, *side)


def _mixer_kernel(x_ref, xres_ref, qk_ref, vo_ref, gmix_ref, win_ref, wmap_ref, bmap_ref, pscale_ref,
                  wdw_ref, bdw_ref, lng_ref, lnb_ref, wout_ref, gattn_ref,
                  wg32_ref, wu32_ref, wd32_ref,
                  x2_ref, pst_ref, cst_ref, wg16_ref, wu16_ref, wd16_ref,
                  a_ext, g_ext, conv_buf, mix_buf, *, tm, nj, n_tiles):
    s = pl.program_id(0)
    j = s % nj

    @pl.when(s == 0)
    def _():
        mix_buf[...] = jnp.zeros(mix_buf.shape, BF16)

    @pl.when(j == 0)
    def _():
        a_ext[:, 0:POOL_PAD, :] = jnp.zeros((POOL_SLABS, POOL_PAD, LANES), F32)
        g_ext[:, 0:CONV_PAD, :] = jnp.zeros((CONV_SLABS, CONV_PAD, LANES), F32)

    def project_and_attend():
        x1 = xres_ref[...] + _dot(mix_buf[...], wout_ref[...])
        sc = _dot(_rms(x1, gattn_ref[...]).astype(BF16), qk_ref[0])
        probs = []
        for hd in range(MEM_HEADS):
            sh = sc[:, hd * N_MEM:(hd + 1) * N_MEM]
            e = jnp.exp2(sh - jnp.max(sh, axis=-1, keepdims=True))
            probs.append((e * (1.0 / jnp.sum(e, axis=-1, keepdims=True))).astype(BF16))
        x2_ref[...] = x1 + _dot(jnp.concatenate(probs, axis=-1), vo_ref[0])

    @pl.when(s < n_tiles)
    def _():
        wg16_ref[...] = wg32_ref[...].astype(BF16)
        wu16_ref[...] = wu32_ref[...].astype(BF16)
        wd16_ref[...] = wd32_ref[...].astype(BF16)

        h = _rms(x_ref[...], gmix_ref[...]).astype(BF16)
        for c in range(CONV_SLABS):
            u = _dot(h, win_ref[:, 2 * c * LANES:2 * (c + 1) * LANES])
            g_ext[c, CONV_PAD:CONV_PAD + tm, :] = u[:, :LANES] * jax.nn.sigmoid(u[:, LANES:])
        u = _dot(h, win_ref[:, 2 * CONV_CH:])
        for c in range(POOL_SLABS):
            a_ext[c, POOL_PAD:POOL_PAD + tm, :] = u[:, c * LANES:(c + 1) * LANES]

        project_and_attend()

        pos = j * tm + lax.broadcasted_iota(jnp.int32, (tm, 1), 0)
        ds = []
        for g, w in enumerate(POOL_WINDOWS):
            a_g = a_ext[g, POOL_PAD:POOL_PAD + tm, :]
            win = a_g
            for i in range(1, w):
                win = win + a_ext[g, POOL_PAD - i:POOL_PAD - i + tm, :]
            inv_cnt = 1.0 / jnp.minimum(w, pos + 1).astype(F32)
            ds.append(win * inv_cnt - a_g)
        ya = _pool_map(jnp.concatenate(ds, axis=-1), wmap_ref, bmap_ref[...], pscale_ref[...])

        for c in range(CONV_SLABS):
            for t0 in range(0, tm, CONV_ROWS):
                acc = None
                for k in range(CONV_K):
                    r0 = CONV_PAD - CONV_HIST + t0 + k
                    g = g_ext[c, r0:r0 + CONV_ROWS, :].reshape(CONV_ROWS // SUBLANES, SUBLANES, LANES)
                    term = g * wdw_ref[k, :, c * LANES:(c + 1) * LANES][None]
                    acc = term if acc is None else acc + term
                conv_buf[t0:t0 + CONV_ROWS, c * LANES:(c + 1) * LANES] = acc.reshape(CONV_ROWS, LANES)
        yb = _layernorm_silu(conv_buf[...] + bdw_ref[...], lng_ref[...], lnb_ref[...])
        mix_buf[...] = jnp.concatenate([ya, yb], axis=-1).astype(BF16)

        for c in range(POOL_SLABS):
            pst_ref[0, :, c * LANES:(c + 1) * LANES] = a_ext[c, POOL_PAD + tm - POOL_HIST:POOL_PAD + tm, :]
            a_ext[c, 0:POOL_PAD, :] = a_ext[c, tm:tm + POOL_PAD, :]
        for c in range(CONV_SLABS):
            cst_ref[0, :, c * LANES:(c + 1) * LANES] = g_ext[c, CONV_PAD + tm - CONV_HIST:CONV_PAD + tm, :]
            g_ext[c, 0:CONV_PAD, :] = g_ext[c, tm:tm + CONV_PAD, :]

    @pl.when(s == n_tiles)
    def _():
        project_and_attend()


def _mixer(x, qk, vo, w, ffn_w, tm):
    nb, seq, _ = x.shape
    nj = seq // tm
    n_tiles = nb * nj
    x2d = x.reshape(nb * seq, D_MODEL)
    cur = lambda s: jnp.minimum(s, n_tiles - 1)
    prev = lambda s: jnp.maximum(s - 1, 0)
    row_cur = pl.BlockSpec((tm, D_MODEL), lambda s: (cur(s), 0))
    row_prev = pl.BlockSpec((tm, D_MODEL), lambda s: (prev(s), 0))
    per_b = lambda shape, tile: pl.BlockSpec(shape, lambda s: (tile(s) // nj, 0, 0))
    consts = [w["g_mix"], w["w_in"], w["wmap"], w["b_map"], w["p_scale"], w["w_dw"], w["b_dw"],
              w["ln_g"], w["ln_b"], w["w_out"], w["g_attn"]]
    ffn32 = [ffn_w["w_gate"], ffn_w["w_up"], ffn_w["w_down"]]
    ffn_specs = [pl.BlockSpec((t.shape[0] // n_tiles, t.shape[1]), lambda s: (cur(s), 0)) for t in ffn32]
    return pl.pallas_call(
        functools.partial(_mixer_kernel, tm=tm, nj=nj, n_tiles=n_tiles),
        grid=(n_tiles + 1,),
        in_specs=[row_cur, row_prev, per_b((1, D_MODEL, MEM_HEADS * N_MEM), prev),
                  per_b((1, MEM_HEADS * N_MEM, D_MODEL), prev)]
                 + [_const_spec(c.shape) for c in consts] + ffn_specs,
        out_specs=[row_prev, per_b((1, POOL_HIST, POOL_WIDTH), cur),
                   per_b((1, CONV_HIST, CONV_CH), cur)] + ffn_specs,
        out_shape=[jax.ShapeDtypeStruct((nb * seq, D_MODEL), F32),
                   jax.ShapeDtypeStruct((nb, POOL_HIST, POOL_WIDTH), F32),
                   jax.ShapeDtypeStruct((nb, CONV_HIST, CONV_CH), F32)]
                  + [jax.ShapeDtypeStruct(t.shape, BF16) for t in ffn32],
        scratch_shapes=[pltpu.VMEM((POOL_SLABS, POOL_PAD + tm, LANES), F32),
                        pltpu.VMEM((CONV_SLABS, CONV_PAD + tm, LANES), F32),
                        pltpu.VMEM((tm, CONV_CH), F32),
                        pltpu.VMEM((tm, D_MODEL), BF16)],
        compiler_params=pltpu.CompilerParams(dimension_semantics=("arbitrary",),
                                             vmem_limit_bytes=VMEM_LIMIT),
        name="mixer",
    )(x2d, x2d, qk, vo, *consts, *ffn32)


def _sample_pre_kernel(x_ref, sp_ref, sc_ref, gmix_ref, win_ref, wmap_ref, bmap_ref, pscale_ref,
                       wdw_ref, bdw_ref, lng_ref, lnb_ref, wout_ref, gattn_ref, wq_ref,
                       x1_ref, q_ref, pst_ref, cst_ref):
    n = x1_ref.shape[0]
    x = _load_row_major(x_ref, n)
    h = _rms(x, gmix_ref[...]).astype(BF16)
    u = _dot(h, win_ref[...])
    a = u[:, 2 * CONV_CH:]
    glu = jnp.concatenate([u[:, 2 * c * LANES:(2 * c + 1) * LANES] * jax.nn.sigmoid(u[:, (2 * c + 1) * LANES:2 * (c + 1) * LANES])
                           for c in range(CONV_SLABS)], axis=-1)

    ds = []
    for g, w in enumerate(POOL_WINDOWS):
        c0 = g * POOL_GROUP_W
        a_g = a[:, c0:c0 + POOL_GROUP_W]
        win = a_g
        for i in range(1, w):
            win = win + sp_ref[POOL_HIST - i, :, c0:c0 + POOL_GROUP_W]
        ds.append(win * (1.0 / w) - a_g)
    ya = _pool_map(jnp.concatenate(ds, axis=-1), wmap_ref, bmap_ref[...], pscale_ref[...])

    conv = glu * wdw_ref[CONV_HIST, 0:1, :] + bdw_ref[...]
    for k in range(CONV_HIST):
        conv = conv + sc_ref[k] * wdw_ref[k, 0:1, :]
    yb = _layernorm_silu(conv, lng_ref[...], lnb_ref[...])

    pst_ref[0:POOL_HIST - 1] = sp_ref[1:POOL_HIST]
    pst_ref[POOL_HIST - 1] = a
    cst_ref[0:CONV_HIST - 1] = sc_ref[1:CONV_HIST]
    cst_ref[CONV_HIST - 1] = glu

    mix = jnp.concatenate([ya, yb], axis=-1).astype(BF16)
    x1 = x + _dot(mix, wout_ref[...])
    x1_ref[...] = x1
    q = _dot(_rms(x1, gattn_ref[...]).astype(BF16), wq_ref[...]) * Q_SCALE
    for r in range(HEAD_ROWS):
        c0 = _head_row_cols(r)
        q_ref[pl.ds(r, n, stride=HEAD_ROWS), :] = q[:, c0:c0 + LANES]


def _sample_pre(x, sp, sc, w):
    n = sp.shape[1]
    nc = n // SAMPLE_CHUNKS
    consts = [w["g_mix"], w["w_in"], w["wmap"], w["b_map"], w["p_scale"], w["w_dw"], w["b_dw"],
              w["ln_g"], w["ln_b"], w["w_out"], w["g_attn"], w["w_q"]]
    rows = lambda r, width: pl.BlockSpec((r, width), lambda i: (i, 0))
    hist = lambda h, width: pl.BlockSpec((h, nc, width), lambda i: (0, i, 0))
    return pl.pallas_call(
        _sample_pre_kernel,
        grid=(SAMPLE_CHUNKS,),
        in_specs=[rows(nc * SUBLANES, LANES), hist(POOL_HIST, POOL_WIDTH), hist(CONV_HIST, CONV_CH)]
                 + [_const_spec(t.shape) for t in consts],
        out_specs=[rows(nc, D_MODEL), rows(nc * HEAD_ROWS, LANES),
                   hist(POOL_HIST, POOL_WIDTH), hist(CONV_HIST, CONV_CH)],
        out_shape=[jax.ShapeDtypeStruct((n, D_MODEL), F32), jax.ShapeDtypeStruct((n * HEAD_ROWS, LANES), F32),
                   jax.ShapeDtypeStruct((POOL_HIST, n, POOL_WIDTH), F32),
                   jax.ShapeDtypeStruct((CONV_HIST, n, CONV_CH), F32)],
        compiler_params=pltpu.CompilerParams(dimension_semantics=("arbitrary",),
                                             vmem_limit_bytes=VMEM_LIMIT),
        name="sample_pre",
    )(x, sp, sc, *consts)


def _attend_one(q, k, v):
    kq = k * q[None]
    s = jnp.sum(kq + pltpu.roll(kq, MEM_HEADS, axis=1), axis=-1, keepdims=True)
    e = jnp.exp2(s - jnp.max(s, axis=0, keepdims=True))
    return jnp.sum(e * v, axis=0) / jnp.sum(e, axis=0)


def _ffn(x2, gffn_ref, wg_ref, wu_ref, wd_ref, gfin_ref):
    h = _rms(x2, gffn_ref[...]).astype(BF16)
    act = (_silu(_dot(h, wg_ref[...])) * _dot(h, wu_ref[...])).astype(BF16)
    x3 = x2 + _dot(act, wd_ref[...])
    return _rms(x3, gfin_ref[...])


def _ffn_attn_kernel(x2_ref, gffn_ref, wg_ref, wu_ref, wd_ref, gfin_ref, q_ref, k_ref, v_ref, x1s_ref, wo_ref,
                     y_ref, ys_ref, os_all, *, sb, steps):
    i = pl.program_id(0)
    for t in range(sb):
        row0 = pl.multiple_of((i * sb + t) * HEAD_ROWS, HEAD_ROWS)
        os_all[pl.ds(row0, HEAD_ROWS), :] = _attend_one(q_ref[t], k_ref[t], v_ref[t])
    y_ref[...] = _ffn(x2_ref[...], gffn_ref, wg_ref, wu_ref, wd_ref, gfin_ref)

    @pl.when(i == steps - 1)
    def _():
        n = x1s_ref.shape[0]
        cols = {_head_row_cols(r): os_all[pl.ds(r, n, stride=HEAD_ROWS), :] for r in range(HEAD_ROWS)}
        o = jnp.concatenate([cols[c] for c in sorted(cols)], axis=-1).astype(BF16)
        x2s = x1s_ref[...] + _dot(o, wo_ref[...])
        _store_row_major(ys_ref, _ffn(x2s, gffn_ref, wg_ref, wu_ref, wd_ref, gfin_ref))


def _to_head_rows(t):
    lead = t.shape[:-2]
    t = t.reshape(lead + (MEM_HEADS, 2, LANES))
    return jnp.swapaxes(t, -3, -2).reshape(lead + (HEAD_ROWS, LANES))


def _from_head_rows(t):
    lead = t.shape[:-2]
    t = t.reshape(lead + (2, MEM_HEADS, LANES))
    return jnp.swapaxes(t, -3, -2).reshape(lead + (MEM_HEADS, MEM_HEAD_DIM))


def _ffn_attn(x2, w, tm, q8, k, v, x1s):
    rows = x2.shape[0]
    steps = rows // tm
    n = x1s.shape[0]
    sb = n // steps
    assert sb * steps == n
    row = pl.BlockSpec((tm, D_MODEL), lambda i: (i, 0))
    consts = [w["g_ffn"], w["w_gate"], w["w_up"], w["w_down"], w["g_final"]]
    blk = pl.BlockSpec((sb, N_MEM, HEAD_ROWS, LANES), lambda i: (i, 0, 0, 0))
    vec = pl.BlockSpec((sb, HEAD_ROWS, LANES), lambda i: (i, 0, 0))
    return pl.pallas_call(
        functools.partial(_ffn_attn_kernel, sb=sb, steps=steps),
        grid=(steps,),
        in_specs=[row] + [_const_spec(c.shape) for c in consts] + [vec, blk, blk]
                 + [_const_spec(x1s.shape), _const_spec(w["w_o"].shape)],
        out_specs=[row, _const_spec((n * SUBLANES, LANES))],
        out_shape=[jax.ShapeDtypeStruct((rows, D_MODEL), F32), jax.ShapeDtypeStruct((n * SUBLANES, LANES), F32)],
        scratch_shapes=[pltpu.VMEM((n * HEAD_ROWS, LANES), F32)],
        compiler_params=pltpu.CompilerParams(dimension_semantics=("arbitrary",), vmem_limit_bytes=VMEM_LIMIT),
        name="ffn_attn",
    )(x2, *consts, q8.reshape(n, HEAD_ROWS, LANES), _to_head_rows(k), _to_head_rows(v), x1s, w["w_o"])


def _prep_weights(g_mix, w_in, pool_map_w, pool_map_b, pool_scale, conv_dw_w, conv_dw_b, conv_ln_g,
                  conv_ln_b, w_out, g_attn, g_mem, w_q, w_k, w_v, w_o, g_ffn, w_gate, w_up, w_down,
                  g_final, l):
    vec = lambda v: v.reshape(1, -1)
    return dict(
        g_mix=vec(g_mix[l]), b_map=vec(pool_map_b[l]), p_scale=vec(pool_scale[l]),
        b_dw=vec(conv_dw_b[l]), ln_g=vec(conv_ln_g[l]), ln_b=vec(conv_ln_b[l]),
        g_attn=vec(g_attn[l]), g_mem=vec(g_mem[l]), g_ffn=vec(g_ffn[l]), g_final=vec(g_final))


def kernel(x_prompt, x_sample, mem_prompt, state_pool, state_conv, cache_mem_k, cache_mem_v, g_mix, w_in,
           pool_map_w, pool_map_b, pool_scale, conv_dw_w, conv_dw_b, conv_ln_g, conv_ln_b, w_out, g_attn,
           g_mem, w_q, w_k, w_v, w_o, g_ffn, w_gate, w_up, w_down, g_final):
    assert state_pool.shape[0] == 1, "single-layer trunk"
    nb, seq, _ = x_prompt.shape
    ns = x_sample.shape[0]
    w = _prep_weights(g_mix, w_in, pool_map_w, pool_map_b, pool_scale, conv_dw_w, conv_dw_b, conv_ln_g,
                      conv_ln_b, w_out, g_attn, g_mem, w_q, w_k, w_v, w_o, g_ffn, w_gate, w_up, w_down,
                      g_final, 0)

    mk, mv, qk, vo, w["wmap"], w["w_dw"], w["w_q"], w["w_o"], w["w_in"], w["w_out"] = _mem_kv(
        mem_prompt, w["g_mem"], w_k[0], w_v[0], w_q[0], w_o[0], pool_map_w[0], conv_dw_w[0], [w_in[0], w_out[0]],
        MEM_BATCH_ROWS)

    x1s, qs, pool_s, conv_s = _sample_pre(x_sample.reshape(ns * SUBLANES, LANES), jnp.swapaxes(state_pool[0], 0, 1),
                                          jnp.swapaxes(state_conv[0], 0, 1), w)
    pool_s = jnp.swapaxes(pool_s, 0, 1)
    conv_s = jnp.swapaxes(conv_s, 0, 1)

    ffn_f32 = dict(w_gate=w_gate[0], w_up=w_up[0], w_down=w_down[0])
    x2p, pool_p, conv_p, w["w_gate"], w["w_up"], w["w_down"] = _mixer(x_prompt, qk, vo, w, ffn_f32, MIXER_ROWS)
    yp, ys = _ffn_attn(x2p, w, FFN_ROWS, qs, cache_mem_k[0], cache_mem_v[0], x1s)

    return (yp.reshape(nb, seq, D_MODEL), ys.reshape(ns, 1, D_MODEL),
            pool_p[None], pool_s[None], conv_p[None], conv_s[None],
            _from_head_rows(mk.reshape(nb, N_MEM, HEAD_ROWS, LANES))[None],
            _from_head_rows(mv.reshape(nb, N_MEM, HEAD_ROWS, LANES))[None])
```
